```python
import math, functools
import jax, jax.numpy as jnp
from jax import lax
import numpy as np

D_MODEL = 1024
BATCH = 4
SEQ = 4096
DEPTH = 2
DEC_BATCH = 128
DEC_SEQ = 8
PAST_LEN = 16384
PAGE_SIZE = 128

DIFF_HEADS = 4
DIFF_KV_HEADS = 2
DIFF_QK_DIM = 32
DIFF_V_DIM = 2 * DIFF_QK_DIM
MLA_HEADS = 8
MLA_Q_RANK = 256
MLA_KV_RANK = 128
MLA_NOPE_DIM = 64
MLA_ROPE_DIM = 32
MLA_V_DIM = 64
FOX_HEADS = 4
FOX_KV_HEADS = 2
FOX_HEAD_DIM = 64
FORGET_BIAS_INIT = 4.0
D_FF = 2816
ROPE_THETA = 10000.0
Q_BLOCK = 128
LN_EPS = 1e-5
RMS_EPS = 1e-6
DEEPNORM_ALPHA = (2 * DEPTH) ** 0.25
DEEPNORM_BETA = (8 * DEPTH) ** -0.25
MLA_SCALE = (MLA_NOPE_DIM + MLA_ROPE_DIM) ** -0.5

DIFF_OUT = DIFF_HEADS * DIFF_V_DIM
MLA_OUT = MLA_HEADS * MLA_V_DIM
FOX_OUT = FOX_HEADS * FOX_HEAD_DIM
MIX_WIDTH = DIFF_OUT + MLA_OUT + FOX_OUT
IN_SIZES = (
    DIFF_HEADS * 2 * DIFF_QK_DIM,
    DIFF_KV_HEADS * 2 * DIFF_QK_DIM,
    DIFF_KV_HEADS * DIFF_V_DIM,
    MLA_Q_RANK,
    MLA_KV_RANK + MLA_ROPE_DIM,
    FOX_HEADS * FOX_HEAD_DIM,
    FOX_KV_HEADS * FOX_HEAD_DIM,
    FOX_KV_HEADS * FOX_HEAD_DIM,
    FOX_HEADS,
)
IN_WIDTH = sum(IN_SIZES)

kernel_name = "hybrid_diff_mla_fox_macaron_deepnorm_step"


def _layer_norm(x, g, b):
    xf = x.astype(jnp.float32)
    mu = jnp.mean(xf, axis=-1, keepdims=True)
    xc = xf - mu
    var = jnp.mean(xc * xc, axis=-1, keepdims=True)
    return (xc * lax.rsqrt(var + LN_EPS) * g.astype(jnp.float32) + b.astype(jnp.float32)).astype(x.dtype)


def _rms_norm(x, g):
    xf = x.astype(jnp.float32)
    return (xf * lax.rsqrt(jnp.mean(xf * xf, axis=-1, keepdims=True) + RMS_EPS) * g.astype(jnp.float32)).astype(x.dtype)


def _rope(x, pos):
    half = x.shape[-1] // 2
    inv_freq = ROPE_THETA ** (-jnp.arange(half, dtype=jnp.float32) / half)
    ang = pos.astype(jnp.float32)[:, None] * inv_freq[None, :]
    cos = jnp.cos(ang)[:, None, :]
    sin = jnp.sin(ang)[:, None, :]
    xf = x.astype(jnp.float32)
    x1, x2 = xf[..., :half], xf[..., half:]
    return jnp.concatenate([x1 * cos - x2 * sin, x2 * cos + x1 * sin], axis=-1).astype(x.dtype)


def _swiglu(x, w1, w3, w2):
    return (jax.nn.silu(x @ w1) * (x @ w3)) @ w2


def _causal_mask(qpos, kpos):
    return kpos[None, :] <= qpos[:, None]


def _diff_core(q, k, v, qpos, kpos, lam):
    b, tq = q.shape[:2]
    tk = k.shape[1]
    g = DIFF_HEADS // DIFF_KV_HEADS
    qg = q.reshape(b, tq, DIFF_KV_HEADS, g, 2, DIFF_QK_DIM)
    kg = k.reshape(b, tk, DIFF_KV_HEADS, 2, DIFF_QK_DIM)
    s = jnp.einsum("bqkgmd,bskmd->bkgmqs", qg, kg, preferred_element_type=jnp.float32) * (DIFF_QK_DIM ** -0.5)
    s = jnp.where(_causal_mask(qpos, kpos), s, -jnp.inf)
    p = jax.nn.softmax(s, axis=-1)
    a = p[:, :, :, 0] - lam * p[:, :, :, 1]
    o = jnp.einsum("bkgqs,bskv->bqkgv", a.astype(v.dtype), v)
    return o.reshape(b, tq, DIFF_HEADS, DIFF_V_DIM)


def _mla_core(q_lat, q_rope, ckv, krope, qpos, kpos):
    s = (jnp.einsum("bqhr,bsr->bhqs", q_lat, ckv, preferred_element_type=jnp.float32)
         + jnp.einsum("bqhe,bse->bhqs", q_rope, krope, preferred_element_type=jnp.float32)) * MLA_SCALE
    s = jnp.where(_causal_mask(qpos, kpos), s, -jnp.inf)
    p = jax.nn.softmax(s, axis=-1)
    return jnp.einsum("bhqs,bsr->bqhr", p.astype(ckv.dtype), ckv)


def _fox_core(q, cq, k, v, ck, qpos, kpos):
    b, tq = q.shape[:2]
    tk = k.shape[1]
    g = FOX_HEADS // FOX_KV_HEADS
    qg = q.reshape(b, tq, FOX_KV_HEADS, g, FOX_HEAD_DIM)
    s = jnp.einsum("bqkgd,bskd->bkgqs", qg, k, preferred_element_type=jnp.float32) * (FOX_HEAD_DIM ** -0.5)
    bias = (cq.reshape(b, tq, FOX_KV_HEADS, g).transpose(0, 2, 3, 1)[..., :, None]
            - ck.reshape(b, tk, FOX_KV_HEADS, g).transpose(0, 2, 3, 1)[..., None, :])
    s = jnp.where(_causal_mask(qpos, kpos), s + bias, -jnp.inf)
    p = jax.nn.softmax(s, axis=-1)
    o = jnp.einsum("bkgqs,bskd->bqkgd", p.astype(v.dtype), v)
    return o.reshape(b, tq, FOX_HEADS, FOX_HEAD_DIM)


def _attend(core, q_args, kv_args, qpos, kpos):
    tq = q_args[0].shape[1]
    if tq > Q_BLOCK and tq % Q_BLOCK == 0:
        nb = tq // Q_BLOCK
        to_blocks = lambda a: jnp.moveaxis(a.reshape(a.shape[0], nb, Q_BLOCK, *a.shape[2:]), 1, 0)
        qb = tuple(to_blocks(a) for a in q_args)
        pb = qpos.reshape(nb, Q_BLOCK)
        out = lax.map(lambda xs: core(*xs[0], *kv_args, xs[1], kpos), (qb, pb))
        out = jnp.moveaxis(out, 0, 1)
        return out.reshape(out.shape[0], tq, *out.shape[3:])
    return core(*q_args, *kv_args, qpos, kpos)


def _gather_past(cache, l, page_table):
    past = cache[l, page_table]
    return past.reshape(page_table.shape[0], page_table.shape[1] * PAGE_SIZE, *past.shape[3:])


def _with_past(new, cache, l, page_table):
    if cache is None:
        return new
    return jnp.concatenate([_gather_past(cache, l, page_table).astype(new.dtype), new], axis=1)


def _token_mixing(xn, pos, l, caches, page_table, w_in, b_forget, diff_lambda, diff_subln_g,
                  mla_q_norm_g, w_uq, mla_kv_norm_g, w_uk, w_uv, mla_out_g, fox_out_g, w_o):
    b, t, _ = xn.shape
    c_dk, c_dv, c_ckv, c_krope, c_fk, c_fv, c_logf = caches
    if c_logf is None:
        kpos = pos
    else:
        kpos = jnp.concatenate([jnp.arange(page_table.shape[1] * PAGE_SIZE, dtype=jnp.int32), pos])
    offs = [int(v) for v in np.cumsum(IN_SIZES)[:-1]]
    dq, dk, dv, mq, mkv, fq, fk, fv, fgate = jnp.split(xn @ w_in, offs, axis=-1)

    dq = _rope(dq.reshape(b, t, DIFF_HEADS * 2, DIFF_QK_DIM), pos).reshape(b, t, DIFF_HEADS, 2, DIFF_QK_DIM)
    dk = _rope(dk.reshape(b, t, DIFF_KV_HEADS * 2, DIFF_QK_DIM), pos).reshape(b, t, DIFF_KV_HEADS, 2 * DIFF_QK_DIM)
    dv = dv.reshape(b, t, DIFF_KV_HEADS, DIFF_V_DIM)
    lam_init = 0.8 - 0.6 * math.exp(-0.3 * l)
    lf = diff_lambda.astype(jnp.float32)
    lam = jnp.exp(jnp.sum(lf[0] * lf[1])) - jnp.exp(jnp.sum(lf[2] * lf[3])) + lam_init
    o = _attend(functools.partial(_diff_core, lam=lam), (dq,),
                (_with_past(dk, c_dk, l, page_table), _with_past(dv, c_dv, l, page_table)), pos, kpos)
    y_diff = (_rms_norm(o, diff_subln_g) * (1.0 - lam_init)).reshape(b, t, DIFF_OUT)

    q = (_rms_norm(mq, mla_q_norm_g) @ w_uq).reshape(b, t, MLA_HEADS, MLA_NOPE_DIM + MLA_ROPE_DIM)
    q_nope = q[..., :MLA_NOPE_DIM]
    q_rope = _rope(q[..., MLA_NOPE_DIM:], pos)
    q_lat = jnp.einsum("bthn,rhn->bthr", q_nope, w_uk.reshape(MLA_KV_RANK, MLA_HEADS, MLA_NOPE_DIM))
    ckv = _rms_norm(mkv[..., :MLA_KV_RANK], mla_kv_norm_g)
    krope = _rope(mkv[..., MLA_KV_RANK:][:, :, None, :], pos)[:, :, 0]
    o_lat = _attend(_mla_core, (q_lat, q_rope),
                    (_with_past(ckv, c_ckv, l, page_table), _with_past(krope, c_krope, l, page_table)), pos, kpos)
    o = jnp.einsum("bthr,rhv->bthv", o_lat, w_uv.reshape(MLA_KV_RANK, MLA_HEADS, MLA_V_DIM))
    y_mla = _rms_norm(o.reshape(b, t, MLA_OUT), mla_out_g)

    fq = fq.reshape(b, t, FOX_HEADS, FOX_HEAD_DIM)
    fk = fk.reshape(b, t, FOX_KV_HEADS, FOX_HEAD_DIM)
    fv = fv.reshape(b, t, FOX_KV_HEADS, FOX_HEAD_DIM)
    logf = jax.nn.log_sigmoid((fgate + b_forget).astype(jnp.float32))
    c_new = lax.cumsum(logf, axis=1)
    if c_logf is None:
        ck = c_new
    else:
        past_logf = _gather_past(c_logf, l, page_table).astype(jnp.float32)
        ck = jnp.concatenate([past_logf - lax.cumsum(past_logf, axis=1, reverse=True), c_new], axis=1)
    o = _attend(_fox_core, (fq, c_new),
                (_with_past(fk, c_fk, l, page_table), _with_past(fv, c_fv, l, page_table), ck), pos, kpos)
    y_fox = _rms_norm(o.reshape(b, t, FOX_OUT), fox_out_g)

    y = jnp.concatenate([y_diff, y_mla, y_fox], axis=-1) @ w_o
    return y, (dk, dv, ckv, krope, fk, fv, logf.astype(xn.dtype))


def _trunk(x, pos, caches, page_table, ln_g, ln_b, ffn_w1, ffn_w3, ffn_w2, w_in, b_forget, diff_lambda,
           diff_subln_g, mla_q_norm_g, w_uq, mla_kv_norm_g, w_uk, w_uv, mla_out_g, fox_out_g, w_o):
    rows = []
    for l in range(DEPTH):
        x = _layer_norm(DEEPNORM_ALPHA * x + 0.5 * _swiglu(x, ffn_w1[l, 0], ffn_w3[l, 0], ffn_w2[l, 0]),
                        ln_g[l, 0], ln_b[l, 0])
        y, new_rows = _token_mixing(x, pos, l, caches, page_table, w_in[l], b_forget[l], diff_lambda[l],
                                    diff_subln_g[l], mla_q_norm_g[l], w_uq[l], mla_kv_norm_g[l], w_uk[l],
                                    w_uv[l], mla_out_g[l], fox_out_g[l], w_o[l])
        x = _layer_norm(DEEPNORM_ALPHA * x + y, ln_g[l, 1], ln_b[l, 1])
        x = _layer_norm(DEEPNORM_ALPHA * x + 0.5 * _swiglu(x, ffn_w1[l, 1], ffn_w3[l, 1], ffn_w2[l, 1]),
                        ln_g[l, 2], ln_b[l, 2])
        rows.append(new_rows)
    new_state = tuple(jnp.stack([r[i] for r in rows], axis=0) for i in range(len(rows[0])))
    return x, new_state


def setup_inputs(seed: int = 0) -> dict:
    key = jax.random.key(seed)
    ks = jax.random.split(key, 32)
    f32 = jnp.float32
    n_pages = PAST_LEN // PAGE_SIZE
    n_pool = (DEC_BATCH * n_pages * 5) // 4
    nrm = lambda k, shape: jax.random.normal(k, shape, f32)

    x_prompt = nrm(ks[0], (BATCH, SEQ, D_MODEL))
    x_sample = nrm(ks[1], (DEC_BATCH, DEC_SEQ, D_MODEL))
    cache_diff_k = nrm(ks[2], (DEPTH, n_pool, PAGE_SIZE, DIFF_KV_HEADS, 2 * DIFF_QK_DIM))
    cache_diff_v = nrm(ks[3], (DEPTH, n_pool, PAGE_SIZE, DIFF_KV_HEADS, DIFF_V_DIM))
    cache_mla_ckv = nrm(ks[4], (DEPTH, n_pool, PAGE_SIZE, MLA_KV_RANK))
    cache_mla_krope = nrm(ks[5], (DEPTH, n_pool, PAGE_SIZE, MLA_ROPE_DIM))
    cache_fox_k = nrm(ks[6], (DEPTH, n_pool, PAGE_SIZE, FOX_KV_HEADS, FOX_HEAD_DIM))
    cache_fox_v = nrm(ks[7], (DEPTH, n_pool, PAGE_SIZE, FOX_KV_HEADS, FOX_HEAD_DIM))
    cache_fox_logf = jax.nn.log_sigmoid(FORGET_BIAS_INIT + nrm(ks[8], (DEPTH, n_pool, PAGE_SIZE, FOX_HEADS)))
    page_table = jax.random.permutation(ks[9], n_pool)[:DEC_BATCH * n_pages].reshape(DEC_BATCH, n_pages).astype(jnp.int32)

    ln_g = 1.0 + 0.02 * nrm(ks[10], (DEPTH, 3, D_MODEL))
    ln_b = 0.02 * nrm(ks[11], (DEPTH, 3, D_MODEL))
    ffn_w1 = nrm(ks[12], (DEPTH, 2, D_MODEL, D_FF)) * (D_MODEL ** -0.5) * DEEPNORM_BETA
    ffn_w3 = nrm(ks[13], (DEPTH, 2, D_MODEL, D_FF)) * (D_MODEL ** -0.5) * DEEPNORM_BETA
    ffn_w2 = nrm(ks[14], (DEPTH, 2, D_FF, D_MODEL)) * (D_FF ** -0.5) * DEEPNORM_BETA
    seg_scale = (1.0, 1.0, DEEPNORM_BETA, 1.0, 1.0, 1.0, 1.0, DEEPNORM_BETA, 1.0)
    col_scale = jnp.concatenate([jnp.full((n,), s, f32) for n, s in zip(IN_SIZES, seg_scale)])
    w_in = nrm(ks[15], (DEPTH, D_MODEL, IN_WIDTH)) * (D_MODEL ** -0.5) * col_scale
    b_forget = FORGET_BIAS_INIT + 0.1 * nrm(ks[16], (DEPTH, FOX_HEADS))
    diff_lambda = 0.1 * nrm(ks[17], (DEPTH, 4, DIFF_QK_DIM))
    diff_subln_g = 1.0 + 0.02 * nrm(ks[18], (DEPTH, DIFF_V_DIM))
    mla_q_norm_g = 1.0 + 0.02 * nrm(ks[19], (DEPTH, MLA_Q_RANK))
    w_uq = nrm(ks[20], (DEPTH, MLA_Q_RANK, MLA_HEADS * (MLA_NOPE_DIM + MLA_ROPE_DIM))) * (MLA_Q_RANK ** -0.5)
    mla_kv_norm_g = 1.0 + 0.02 * nrm(ks[21], (DEPTH, MLA_KV_RANK))
    w_uk = nrm(ks[22], (DEPTH, MLA_KV_RANK, MLA_HEADS * MLA_NOPE_DIM)) * (MLA_KV_RANK ** -0.5)
    w_uv = nrm(ks[23], (DEPTH, MLA_KV_RANK, MLA_HEADS * MLA_V_DIM)) * (MLA_KV_RANK ** -0.5) * DEEPNORM_BETA
    mla_out_g = 1.0 + 0.02 * nrm(ks[24], (DEPTH, MLA_OUT))
    fox_out_g = 1.0 + 0.02 * nrm(ks[25], (DEPTH, FOX_OUT))
    w_o = nrm(ks[26], (DEPTH, MIX_WIDTH, D_MODEL)) * (MIX_WIDTH ** -0.5) * DEEPNORM_BETA
    return {
        "x_prompt": x_prompt, "x_sample": x_sample,
        "cache_diff_k": cache_diff_k, "cache_diff_v": cache_diff_v,
        "cache_mla_ckv": cache_mla_ckv, "cache_mla_krope": cache_mla_krope,
        "cache_fox_k": cache_fox_k, "cache_fox_v": cache_fox_v, "cache_fox_logf": cache_fox_logf,
        "page_table": page_table,
        "ln_g": ln_g, "ln_b": ln_b, "ffn_w1": ffn_w1, "ffn_w3": ffn_w3, "ffn_w2": ffn_w2,
        "w_in": w_in, "b_forget": b_forget, "diff_lambda": diff_lambda, "diff_subln_g": diff_subln_g,
        "mla_q_norm_g": mla_q_norm_g, "w_uq": w_uq, "mla_kv_norm_g": mla_kv_norm_g, "w_uk": w_uk,
        "w_uv": w_uv, "mla_out_g": mla_out_g, "fox_out_g": fox_out_g, "w_o": w_o,
    }


def reference(x_prompt, x_sample, cache_diff_k, cache_diff_v, cache_mla_ckv, cache_mla_krope,
              cache_fox_k, cache_fox_v, cache_fox_logf, page_table, ln_g, ln_b, ffn_w1, ffn_w3, ffn_w2,
              w_in, b_forget, diff_lambda, diff_subln_g, mla_q_norm_g, w_uq, mla_kv_norm_g, w_uk, w_uv,
              mla_out_g, fox_out_g, w_o):
    weights = (ln_g, ln_b, ffn_w1, ffn_w3, ffn_w2, w_in, b_forget, diff_lambda, diff_subln_g,
               mla_q_norm_g, w_uq, mla_kv_norm_g, w_uk, w_uv, mla_out_g, fox_out_g, w_o)
    pos_prompt = jnp.arange(x_prompt.shape[1], dtype=jnp.int32)
    y_prompt, prompt_state = _trunk(x_prompt, pos_prompt, (None,) * 7, None, *weights)
    past_len = page_table.shape[1] * PAGE_SIZE
    pos_sample = past_len + jnp.arange(x_sample.shape[1], dtype=jnp.int32)
    caches = (cache_diff_k, cache_diff_v, cache_mla_ckv, cache_mla_krope, cache_fox_k, cache_fox_v, cache_fox_logf)
    y_sample, sample_state = _trunk(x_sample, pos_sample, caches, page_table, *weights)
    p_diff_k, p_diff_v, p_mla_ckv, p_mla_krope, p_fox_k, p_fox_v, p_fox_logf = prompt_state
    s_diff_k, s_diff_v, s_mla_ckv, s_mla_krope, s_fox_k, s_fox_v, s_fox_logf = sample_state
    return (y_prompt, y_sample,
            p_diff_k, p_diff_v, p_mla_ckv, p_mla_krope, p_fox_k, p_fox_v, p_fox_logf,
            s_diff_k, s_diff_v, s_mla_ckv, s_mla_krope, s_fox_k, s_fox_v, s_fox_logf)
```

```python
import functools
import math

import numpy as np
import jax
import jax.numpy as jnp
from jax import lax
from jax.experimental import pallas as pl
from jax.experimental.pallas import tpu as pltpu

F32 = jnp.float32
BF16 = jnp.bfloat16

D_MODEL = 1024
DIFF_HEADS, DIFF_KV_HEADS, DIFF_QK_DIM, DIFF_V_DIM = 4, 2, 32, 64
MLA_HEADS, MLA_Q_RANK, MLA_KV_RANK, MLA_NOPE_DIM, MLA_ROPE_DIM, MLA_V_DIM = 8, 256, 128, 64, 32, 64
FOX_HEADS, FOX_KV_HEADS, FOX_HEAD_DIM = 4, 2, 64
D_FF = 2816
PAGE_SIZE = 128
ROPE_THETA = 10000.0
LN_EPS = 1e-5
RMS_EPS = 1e-6
MLA_SCALE = (MLA_NOPE_DIM + MLA_ROPE_DIM) ** -0.5
DIFF_OUT = DIFF_HEADS * DIFF_V_DIM
MLA_OUT = MLA_HEADS * MLA_V_DIM
FOX_OUT = FOX_HEADS * FOX_HEAD_DIM
MLA_QK = MLA_KV_RANK + MLA_ROPE_DIM

LANES = 128
VMEM_LIMIT_BYTES = 56 * 1024 * 1024

FFN_TM = 512
FFN_CHUNK = 256
MIX_TM = 512
ATT_T = 512
PAGES_PER_CHUNK = 16

SEG_DQ, SEG_DQ_SW, SEG_DK, SEG_DK_SW, SEG_DV = 0, 256, 512, 640, 768
SEG_MQ, SEG_CKV, SEG_KR, SEG_KR_SW, SEG_FG = 896, 1152, 1280, 1408, 1536
SEG_FQ, SEG_FK, SEG_FV, W_AUG_COLS = 1664, 1920, 2048, 2176


def _ln(y, g, b):
    mu = jnp.mean(y, axis=-1, keepdims=True)
    yc = y - mu
    var = jnp.mean(yc * yc, axis=-1, keepdims=True)
    return yc * lax.rsqrt(var + LN_EPS) * g + b


def _rms(x, g):
    return x * lax.rsqrt(jnp.mean(x * x, axis=-1, keepdims=True) + RMS_EPS) * g


def _cparams(sem):
    return pltpu.CompilerParams(dimension_semantics=sem, vmem_limit_bytes=VMEM_LIMIT_BYTES)


def _const_spec(shape):
    nd = len(shape)
    return pl.BlockSpec(shape, lambda *_: (0,) * nd, pipeline_mode=pl.Buffered(1))


def _ffn_ln_kernel(x_ref, w1_ref, w3_ref, w2_ref, g_ref, b_ref, o_ref, *, alpha):
    x = x_ref[...]
    xb = x.astype(BF16)
    acc = jnp.zeros(x.shape, F32)
    for c in range(D_FF // FFN_CHUNK):
        sl = slice(c * FFN_CHUNK, (c + 1) * FFN_CHUNK)
        h1 = jnp.dot(xb, w1_ref[:, sl], preferred_element_type=F32)
        h3 = jnp.dot(xb, w3_ref[:, sl], preferred_element_type=F32)
        h = (h1 * jax.nn.sigmoid(h1) * h3).astype(BF16)
        acc = acc + jnp.dot(h, w2_ref[sl, :], preferred_element_type=F32)
    o_ref[...] = _ln(alpha * x + 0.5 * acc, g_ref[...], b_ref[...])


def _ffn_ln(x, w1, w3, w2, g, b, alpha):
    n = x.shape[0]
    tm = min(FFN_TM, n)
    return pl.pallas_call(
        functools.partial(_ffn_ln_kernel, alpha=alpha),
        grid=(n // tm,),
        in_specs=[
            pl.BlockSpec((tm, D_MODEL), lambda i: (i, 0)),
            _const_spec((D_MODEL, D_FF)), _const_spec((D_MODEL, D_FF)), _const_spec((D_FF, D_MODEL)),
            _const_spec((1, D_MODEL)), _const_spec((1, D_MODEL)),
        ],
        out_specs=pl.BlockSpec((tm, D_MODEL), lambda i: (i, 0)),
        out_shape=jax.ShapeDtypeStruct((n, D_MODEL), F32),
        compiler_params=_cparams(("parallel",)),
        name="ffn_ln",
    )(x, w1, w3, w2, g, b)


def _log_sigmoid(z):
    return jnp.minimum(z, 0.0) - jnp.log1p(jnp.exp(-jnp.abs(z)))


def _mix_in_kernel(x_ref, w_ref, cos_ref, sin_ref, gq_ref, wuq_ref, wukbd_ref, gkv_ref, bf_ref, *refs,
                   transposed, seq_len, tiles_per_seq):
    if transposed:
        (qd_ref, qlat_ref, qrope_ref, fq_ref, dkT_ref, dvT_ref, ckv_ref, krT_ref, fkT_ref, fvT_ref,
         lfT_ref, kdT_ref, vdT_ref, kmT_ref, kfT_ref, vfT_ref, cT_ref, carry_ref) = refs
    else:
        (qd_ref, qlat_ref, qrope_ref, fq_ref, dk_ref, dv_ref, ckv_ref, kr_ref, fk_ref, fv_ref,
         lf_ref, c_ref) = refs
    tm = x_ref.shape[0]
    xb = x_ref[...].astype(BF16)
    proj = jnp.dot(xb, w_ref[...], preferred_element_type=F32)
    cos = cos_ref[...]
    sin = sin_ref[...]
    cos1, sin1 = cos[:, :LANES], sin[:, :LANES]

    dq = proj[:, SEG_DQ:SEG_DQ + 256] * cos + proj[:, SEG_DQ_SW:SEG_DQ_SW + 256] * sin
    dk = proj[:, SEG_DK:SEG_DK + 128] * cos1 + proj[:, SEG_DK_SW:SEG_DK_SW + 128] * sin1
    dv = proj[:, SEG_DV:SEG_DV + 128]
    kr = proj[:, SEG_KR:SEG_KR + 128] * cos1 + proj[:, SEG_KR_SW:SEG_KR_SW + 128] * sin1
    fk = proj[:, SEG_FK:SEG_FK + 128]
    fv = proj[:, SEG_FV:SEG_FV + 128]

    mqn = _rms(proj[:, SEG_MQ:SEG_MQ + MLA_Q_RANK], gq_ref[...]).astype(BF16)
    q2 = jnp.dot(mqn, wuq_ref[...], preferred_element_type=F32)
    qrope = q2[:, 512:768] * cos + q2[:, 768:1024] * sin
    qlat = jnp.dot(q2[:, :512].astype(BF16), wukbd_ref[...], preferred_element_type=F32)
    ckv = _rms(proj[:, SEG_CKV:SEG_CKV + MLA_KV_RANK], gkv_ref[...])

    lane = lax.broadcasted_iota(jnp.int32, (tm, LANES), 1)
    lf = jnp.where(lane < FOX_HEADS, _log_sigmoid(proj[:, SEG_FG:SEG_FG + 128] + bf_ref[...]), 0.0)

    qd_ref[...] = (dq * (DIFF_QK_DIM ** -0.5)).astype(BF16)
    qlat_ref[...] = (qlat * MLA_SCALE).astype(BF16)
    qrope_ref[...] = (qrope * MLA_SCALE).astype(BF16)
    fq_ref[...] = (proj[:, SEG_FQ:SEG_FQ + 256] * (FOX_HEAD_DIM ** -0.5)).astype(BF16)
    ckv_ref[...] = ckv

    if transposed:
        dkT, dvT, fkT, fvT = dk.T, dv.T, fk.T, fv.T
        krT = kr.T[:MLA_ROPE_DIM]
        lfT = lf.T[:8]
        dkT_ref[...] = dkT
        dvT_ref[...] = dvT
        fkT_ref[...] = fkT
        fvT_ref[...] = fvT
        krT_ref[...] = krT
        lfT_ref[...] = lfT[:FOX_HEADS]
        kdT_ref[...] = dkT.astype(BF16)
        vdT_ref[...] = dvT.astype(BF16)
        kfT_ref[...] = fkT.astype(BF16)
        vfT_ref[...] = fvT.astype(BF16)
        kmT_ref[0:MLA_KV_RANK, :] = ckv.T.astype(BF16)
        kmT_ref[MLA_KV_RANK:MLA_QK, :] = krT.astype(BF16)

        @pl.when(pl.program_id(0) % tiles_per_seq == 0)
        def _():
            carry_ref[...] = jnp.zeros_like(carry_ref)
        lane8 = lax.broadcasted_iota(jnp.int32, (8, LANES), 1)
        carry = carry_ref[:, 0:1]
        for p in range(tm // LANES):
            piece = lfT[:, p * LANES:(p + 1) * LANES]
            total = jnp.sum(piece, axis=1, keepdims=True)
            k = 1
            while k < LANES:
                piece = piece + jnp.where(lane8 >= k, pltpu.roll(piece, k, 1), 0.0)
                k *= 2
            piece = piece + carry
            for h in range(FOX_HEADS):
                cT_ref[h, :, p * LANES:(p + 1) * LANES] = piece[h:h + 1, :]
            carry = carry + total
        carry_ref[...] = jnp.broadcast_to(carry, carry_ref.shape)
    else:
        dk_ref[...] = dk
        dv_ref[...] = dv
        fk_ref[...] = fk
        fv_ref[...] = fv
        kr_ref[...] = kr
        lf_ref[...] = lf
        row = lax.broadcasted_iota(jnp.int32, (tm, LANES), 0) % seq_len
        c = lf
        k = 1
        while k < seq_len:
            c = c + jnp.where(row >= k, pltpu.roll(c, k, 0), 0.0)
            k *= 2
        c_ref[...] = c


def _mix_in(x, wts, cos_t, sin_t, *, transposed, batch, seq_len):
    n = x.shape[0]
    tm = min(MIX_TM, n)
    nt = n // tm
    tok = lambda w: pl.BlockSpec((tm, w), lambda i: (i, 0))
    tok_out = lambda w, dt: jax.ShapeDtypeStruct((n, w), dt)
    if transposed:
        tps = seq_len // tm
        tbl = pl.BlockSpec((tm, 256), lambda i: (i % tps, 0))
        featT = lambda r: pl.BlockSpec((None, r, tm), lambda i: (i // tps, 0, i % tps))
        featT_out = lambda r: jax.ShapeDtypeStruct((batch, r, seq_len), F32)
        blkT = lambda r: pl.BlockSpec((None, None, r, tm), lambda i: (i // tps, i % tps, 0, 0))
        blkT_out = lambda r: jax.ShapeDtypeStruct((batch, tps, r, tm), BF16)
        out_specs = [tok(256), tok(1024), tok(256), tok(256),
                     featT(128), featT(128), tok(128), featT(MLA_ROPE_DIM), featT(128), featT(128),
                     featT(FOX_HEADS),
                     blkT(128), blkT(128), blkT(MLA_QK), blkT(128), blkT(128),
                     pl.BlockSpec((None, FOX_HEADS, None, 1, tm), lambda i: (i // tps, 0, i % tps, 0, 0))]
        out_shape = [tok_out(256, BF16), tok_out(1024, BF16), tok_out(256, BF16), tok_out(256, BF16),
                     featT_out(128), featT_out(128), tok_out(128, F32), featT_out(MLA_ROPE_DIM),
                     featT_out(128), featT_out(128), featT_out(FOX_HEADS),
                     blkT_out(128), blkT_out(128), blkT_out(MLA_QK), blkT_out(128), blkT_out(128),
                     jax.ShapeDtypeStruct((batch, FOX_HEADS, tps, 1, tm), F32)]
        scratch = [pltpu.VMEM((8, LANES), F32)]
        sem = ("arbitrary",)
    else:
        tps = 1
        tbl = pl.BlockSpec((tm, 256), lambda i: (0, 0))
        out_specs = [tok(256), tok(1024), tok(256), tok(256)] + [tok(128)] * 8
        out_shape = [tok_out(256, BF16), tok_out(1024, BF16), tok_out(256, BF16), tok_out(256, BF16)] + \
                    [tok_out(128, F32)] * 8
        scratch = []
        sem = ("parallel",)
    w_aug, gq, wuq, wukbd, gkv, bf = wts
    return pl.pallas_call(
        functools.partial(_mix_in_kernel, transposed=transposed, seq_len=seq_len, tiles_per_seq=tps),
        grid=(nt,),
        in_specs=[tok(D_MODEL), _const_spec(w_aug.shape), tbl, tbl, _const_spec(gq.shape),
                  _const_spec(wuq.shape), _const_spec(wukbd.shape), _const_spec(gkv.shape),
                  _const_spec(bf.shape)],
        out_specs=out_specs,
        out_shape=out_shape,
        scratch_shapes=scratch,
        compiler_params=_cparams(sem),
        name="mix_in_prompt" if transposed else "mix_in_sample",
    )(x, w_aug, cos_t, sin_t, gq, wuq, wukbd, gkv, bf)


def _flash_kernel(q_ref, kT_ref, vT_ref, *refs, has_bias):
    if has_bias:
        cT_ref, o_ref = refs
    else:
        (o_ref,) = refs
    i = pl.program_id(2)
    t = q_ref.shape[0]
    dv = vT_ref.shape[1]
    q = q_ref[...]
    row = lax.broadcasted_iota(jnp.int32, (t, t), 0)
    col = lax.broadcasted_iota(jnp.int32, (t, t), 1)
    if has_bias:
        cq = jnp.sum(jnp.where(row == col, cT_ref[i], 0.0), axis=1, keepdims=True)

    def step(j, carry, masked):
        m, l, acc = carry
        s = jnp.dot(q, kT_ref[j], preferred_element_type=F32)
        if has_bias:
            s = s + (cq - cT_ref[j])
        if masked:
            s = jnp.where(col <= row, s, -jnp.inf)
        m_new = jnp.maximum(m, jnp.max(s, axis=1, keepdims=True))
        a = jnp.exp(m - m_new)
        p = jnp.exp(s - m_new)
        l = a * l + jnp.sum(p, axis=1, keepdims=True)
        pv = lax.dot_general(p.astype(BF16), vT_ref[j], (((1,), (1,)), ((), ())),
                             preferred_element_type=F32)
        return m_new, l, a * acc + pv

    init = (jnp.full((t, 1), -jnp.inf, F32), jnp.zeros((t, 1), F32), jnp.zeros((t, dv), F32))
    carry = lax.fori_loop(0, i, functools.partial(step, masked=False), init)
    _, l, acc = step(i, carry, True)
    o_ref[...] = acc / l


def _flash(q, kT, vT, cT, *, k_rows, k_idx, v_rows, v_idx):
    b, hn, seq, dq = q.shape
    nkb, t = kT.shape[1], kT.shape[3]
    in_specs = [
        pl.BlockSpec((None, None, t, dq), lambda bi, h, i: (bi, h, i, 0)),
        pl.BlockSpec((None, nkb, k_rows, t), lambda bi, h, i: (bi, 0, k_idx(h), 0)),
        pl.BlockSpec((None, nkb, v_rows, t), lambda bi, h, i: (bi, 0, v_idx(h), 0)),
    ]
    args = [q, kT, vT]
    if cT is not None:
        in_specs.append(pl.BlockSpec((None, None, nkb, 1, t), lambda bi, h, i: (bi, h, 0, 0, 0)))
        args.append(cT)
    return pl.pallas_call(
        functools.partial(_flash_kernel, has_bias=cT is not None),
        grid=(b, hn, seq // t),
        in_specs=in_specs,
        out_specs=pl.BlockSpec((None, None, t, v_rows), lambda bi, h, i: (bi, h, i, 0)),
        out_shape=jax.ShapeDtypeStruct((b, hn, seq, v_rows), F32),
        compiler_params=_cparams(("parallel", "parallel", "parallel")),
        name="flash_bias" if cT is not None else f"flash_k{k_rows}",
    )(*args)


def _softmax_update(s, state, v_dot):
    m, l, acc = state
    m_new = jnp.maximum(m, jnp.max(s, axis=1, keepdims=True))
    a = jnp.exp(m - m_new)
    p = jnp.exp(s - m_new)
    return m_new, a * l + jnp.sum(p, axis=1, keepdims=True), a * acc + v_dot(p.astype(BF16))


def _nt(a, b):
    return lax.dot_general(a, b, (((1,), (1,)), ((), ())), preferred_element_type=F32)


def _sample_attn_kernel(pt_ref,
                        qd_ref, qm_ref, qf_ref, cqf_ref,
                        dkn_ref, dvn_ref, ckvn_ref, krn_ref, fkn_ref, fvn_ref, ctn_ref,
                        dkc, dvc, ckvc, krc, fkc, fvc, lfc,
                        od_ref, om_ref, of_ref,
                        dk_buf, dv_buf, ckv_buf, kr_buf, fk_buf, fv_buf, lf_buf, sems,
                        *, layer, n_chunks, ppc):
    b = pl.program_id(0)
    nb = pl.num_programs(0)
    chunk_keys = ppc * PAGE_SIZE
    n_dec = ctn_ref.shape[1]

    def page_copies(bi, ci, slot, p, page):
        lanes = pl.ds(p * PAGE_SIZE, PAGE_SIZE)
        return (
            pltpu.make_async_copy(dkc.at[layer, page], dk_buf.at[slot, :, :, lanes], sems.at[slot, 0]),
            pltpu.make_async_copy(dvc.at[layer, page], dv_buf.at[slot, :, :, lanes], sems.at[slot, 1]),
            pltpu.make_async_copy(ckvc.at[layer, page], ckv_buf.at[slot, pl.ds(p * PAGE_SIZE, PAGE_SIZE), :],
                                  sems.at[slot, 2]),
            pltpu.make_async_copy(krc.at[layer, page], kr_buf.at[slot, :, lanes], sems.at[slot, 3]),
            pltpu.make_async_copy(fkc.at[layer, page], fk_buf.at[slot, :, :, lanes], sems.at[slot, 4]),
            pltpu.make_async_copy(fvc.at[layer, page], fv_buf.at[slot, :, :, lanes], sems.at[slot, 5]),
            pltpu.make_async_copy(lfc.at[layer, page], lf_buf.at[slot, :, lanes], sems.at[slot, 6]),
        )

    def start_chunk(bi, ci, slot):
        for p in range(ppc):
            page = pt_ref[bi, ci * ppc + p]
            for cp in page_copies(bi, ci, slot, p, page):
                cp.start()

    def wait_chunk(slot):
        for p in range(ppc):
            for cp in page_copies(0, 0, slot, p, 0):
                cp.wait()

    @pl.when(b == 0)
    def _():
        start_chunk(0, n_chunks - 1, 0)

    qd = [qd_ref[h] for h in range(DIFF_KV_HEADS)]
    qm = qm_ref[...]
    qf = [qf_ref[h] for h in range(FOX_KV_HEADS)]
    cqf = [cqf_ref[h] for h in range(FOX_KV_HEADS)]
    g_fox = FOX_HEADS // FOX_KV_HEADS
    lane4 = lax.broadcasted_iota(jnp.int32, (FOX_HEADS, LANES), 1)

    def init(rows, dv):
        return (jnp.full((rows, 1), -jnp.inf, F32), jnp.zeros((rows, 1), F32), jnp.zeros((rows, dv), F32))

    def fox_bias(h, d_rows):
        return jnp.concatenate(
            [cqf[h][g * n_dec:(g + 1) * n_dec] + d_rows[h * g_fox + g:h * g_fox + g + 1, :] for g in range(g_fox)],
            axis=0)

    def chunk_body(k, carry):
        sd, sm, sf, lf_after = carry
        ci = n_chunks - 1 - k
        slot = k % 2
        wait_chunk(slot)

        @pl.when(k + 1 < n_chunks)
        def _():
            start_chunk(b, ci - 1, 1 - slot)

        @pl.when(jnp.logical_and(k + 1 == n_chunks, b + 1 < nb))
        def _():
            start_chunk(b + 1, n_chunks - 1, 1 - slot)

        new_sd = []
        for h in range(DIFF_KV_HEADS):
            s = jnp.dot(qd[h], dk_buf[slot, h].astype(BF16), preferred_element_type=F32)
            v = dv_buf[slot, h].astype(BF16)
            new_sd.append(_softmax_update(s, sd[h], lambda p, v=v: _nt(p, v)))
        ckv = ckv_buf[slot].astype(BF16)
        s = _nt(qm[:, :MLA_KV_RANK], ckv) + jnp.dot(qm[:, MLA_KV_RANK:], kr_buf[slot].astype(BF16),
                                                   preferred_element_type=F32)
        new_sm = _softmax_update(s, sm, lambda p: jnp.dot(p, ckv, preferred_element_type=F32))
        pieces = [None] * ppc
        for p in reversed(range(ppc)):
            x = lf_buf[slot, :, p * PAGE_SIZE:(p + 1) * PAGE_SIZE]
            total = jnp.sum(x, axis=1, keepdims=True)
            sfx = x
            kk = 1
            while kk < PAGE_SIZE:
                sfx = sfx + jnp.where(lane4 + kk < PAGE_SIZE, pltpu.roll(sfx, PAGE_SIZE - kk, 1), 0.0)
                kk *= 2
            pieces[p] = sfx - x + lf_after
            lf_after = lf_after + total
        d_rows = jnp.concatenate(pieces, axis=1)
        new_sf = []
        for h in range(FOX_KV_HEADS):
            s = jnp.dot(qf[h], fk_buf[slot, h].astype(BF16), preferred_element_type=F32) + fox_bias(h, d_rows)
            v = fv_buf[slot, h].astype(BF16)
            new_sf.append(_softmax_update(s, sf[h], lambda p, v=v: _nt(p, v)))
        return tuple(new_sd), new_sm, tuple(new_sf), lf_after

    rows_d = qd_ref.shape[1]
    rows_f = qf_ref.shape[1]
    carry0 = (tuple(init(rows_d, DIFF_V_DIM) for _ in range(DIFF_KV_HEADS)),
              init(qm_ref.shape[0], MLA_KV_RANK),
              tuple(init(rows_f, FOX_HEAD_DIM) for _ in range(FOX_KV_HEADS)),
              jnp.zeros((FOX_HEADS, 1), F32))
    sd, sm, sf, _ = lax.fori_loop(0, n_chunks, chunk_body, carry0)

    def causal(rows):
        tq = lax.broadcasted_iota(jnp.int32, (rows, n_dec), 0) % n_dec
        tk = lax.broadcasted_iota(jnp.int32, (rows, n_dec), 1)
        return tk <= tq

    for h in range(DIFF_KV_HEADS):
        kn = dkn_ref[:, h * 64:(h + 1) * 64].astype(BF16)
        vn = dvn_ref[:, h * 64:(h + 1) * 64].astype(BF16)
        s = jnp.where(causal(rows_d), _nt(qd[h], kn), -jnp.inf)
        _, l, acc = _softmax_update(s, sd[h], lambda p, vn=vn: jnp.dot(p, vn, preferred_element_type=F32))
        od_ref[h] = acc / l
    ckvn = ckvn_ref[...].astype(BF16)
    s = _nt(qm[:, :MLA_KV_RANK], ckvn) + _nt(qm[:, MLA_KV_RANK:], krn_ref[:, :MLA_ROPE_DIM].astype(BF16))
    s = jnp.where(causal(qm_ref.shape[0]), s, -jnp.inf)
    _, l, acc = _softmax_update(s, sm, lambda p: jnp.dot(p, ckvn, preferred_element_type=F32))
    om_ref[...] = acc / l
    ctn = ctn_ref[...]
    for h in range(FOX_KV_HEADS):
        kn = fkn_ref[:, h * 64:(h + 1) * 64].astype(BF16)
        vn = fvn_ref[:, h * 64:(h + 1) * 64].astype(BF16)
        s = _nt(qf[h], kn) + fox_bias(h, -ctn)
        s = jnp.where(causal(rows_f), s, -jnp.inf)
        _, l, acc = _softmax_update(s, sf[h], lambda p, vn=vn: jnp.dot(p, vn, preferred_element_type=F32))
        of_ref[h] = acc / l


def _sample_attn(layer, page_table, qd, qm, qf, cqf, new_rows, caches):
    nb, n_pages = page_table.shape
    ppc = min(PAGES_PER_CHUNK, n_pages)
    n_chunks = n_pages // ppc
    ck = ppc * PAGE_SIZE
    n_dec = new_rows[0].shape[1]
    per_b = lambda shp: pl.BlockSpec((None,) + shp, lambda b, pt: (b,) + (0,) * len(shp))
    any_spec = pl.BlockSpec(memory_space=pl.ANY)
    rows_d, rows_m, rows_f = qd.shape[2], qm.shape[1], qf.shape[2]
    in_specs = [per_b(qd.shape[1:]), per_b(qm.shape[1:]), per_b(qf.shape[1:]), per_b(cqf.shape[1:])]
    in_specs += [per_b(a.shape[1:]) for a in new_rows]
    in_specs += [any_spec] * 7
    out_shape = [jax.ShapeDtypeStruct((nb, DIFF_KV_HEADS, rows_d, DIFF_V_DIM), F32),
                 jax.ShapeDtypeStruct((nb, rows_m, MLA_KV_RANK), F32),
                 jax.ShapeDtypeStruct((nb, FOX_KV_HEADS, rows_f, FOX_HEAD_DIM), F32)]
    out_specs = [per_b(s.shape[1:]) for s in out_shape]
    scratch = [
        pltpu.VMEM((2, DIFF_KV_HEADS, 64, ck), F32), pltpu.VMEM((2, DIFF_KV_HEADS, 64, ck), F32),
        pltpu.VMEM((2, ck, MLA_KV_RANK), F32), pltpu.VMEM((2, MLA_ROPE_DIM, ck), F32),
        pltpu.VMEM((2, FOX_KV_HEADS, 64, ck), F32), pltpu.VMEM((2, FOX_KV_HEADS, 64, ck), F32),
        pltpu.VMEM((2, FOX_HEADS, ck), F32),
        pltpu.SemaphoreType.DMA((2, 7)),
    ]
    return pl.pallas_call(
        functools.partial(_sample_attn_kernel, layer=layer, n_chunks=n_chunks, ppc=ppc),
        grid_spec=pltpu.PrefetchScalarGridSpec(
            num_scalar_prefetch=1, grid=(nb,), in_specs=in_specs, out_specs=out_specs,
            scratch_shapes=scratch),
        out_shape=out_shape,
        compiler_params=_cparams(("arbitrary",)),
        name="sample_attn",
    )(page_table, qd, qm, qf, cqf, *new_rows, *caches)


def _mix_out_kernel(x_ref, od_ref, om_ref, of_ref, dl_ref, gd_ref, bmd_ref, wuv_ref, gm_ref, gf_ref,
                    wod_ref, wom_ref, wof_ref, g_ref, b_ref, o_ref, *, alpha, lam_init):
    dl = dl_ref[...]
    lam = (jnp.exp(jnp.sum(dl[0:1] * dl[1:2], axis=1, keepdims=True))
           - jnp.exp(jnp.sum(dl[2:3] * dl[3:4], axis=1, keepdims=True)) + lam_init)
    od = od_ref[...]
    d = od - lam * pltpu.roll(od, od.shape[1] - DIFF_V_DIM, 1)
    ms = jnp.dot((d * d).astype(BF16), bmd_ref[...], preferred_element_type=F32)
    yd = d * lax.rsqrt(ms + RMS_EPS) * gd_ref[...]
    ov = jnp.dot(om_ref[...].astype(BF16), wuv_ref[...], preferred_element_type=F32)
    ym = _rms(ov, gm_ref[...])
    yf = _rms(of_ref[...], gf_ref[...])
    y = (jnp.dot(yd.astype(BF16), wod_ref[...], preferred_element_type=F32)
         + jnp.dot(ym.astype(BF16), wom_ref[...], preferred_element_type=F32)
         + jnp.dot(yf.astype(BF16), wof_ref[...], preferred_element_type=F32))
    o_ref[...] = _ln(alpha * x_ref[...] + y, g_ref[...], b_ref[...])


def _mix_out(x, od, om, of, wts, alpha, lam_init):
    n = x.shape[0]
    tm = min(MIX_TM, n)
    tok = lambda w: pl.BlockSpec((tm, w), lambda i: (i, 0))
    return pl.pallas_call(
        functools.partial(_mix_out_kernel, alpha=alpha, lam_init=lam_init),
        grid=(n // tm,),
        in_specs=[tok(D_MODEL), tok(od.shape[1]), tok(om.shape[1]), tok(of.shape[1])]
                 + [_const_spec(w.shape) for w in wts],
        out_specs=tok(D_MODEL),
        out_shape=jax.ShapeDtypeStruct((n, D_MODEL), F32),
        compiler_params=_cparams(("parallel",)),
        name="mix_out",
    )(x, od, om, of, *wts)


def _swap_halves_cols(w, chunk=32):
    c = w.shape[-1]
    w4 = w.reshape(w.shape[:-1] + (c // chunk, 2, chunk // 2))
    return jnp.flip(w4, axis=-2).reshape(w.shape)


def _pad_cols(w, width):
    return jnp.pad(w, [(0, 0)] * (w.ndim - 1) + [(0, width - w.shape[-1])])


def _block_diag(blocks):
    h, r, c = blocks.shape
    eye = jnp.eye(h, dtype=blocks.dtype)
    return (blocks[:, :, None, :] * eye[:, None, :, None]).reshape(h * r, h * c)


def _prep_layer_weights(l, w_in, b_forget, mla_q_norm_g, w_uq, mla_kv_norm_g, w_uk, w_uv, diff_lambda,
                        diff_subln_g, mla_out_g, fox_out_g, w_o, lam_init):
    offs = np.cumsum([0, DIFF_HEADS * 2 * DIFF_QK_DIM, DIFF_KV_HEADS * 2 * DIFF_QK_DIM, DIFF_KV_HEADS * DIFF_V_DIM,
                      MLA_Q_RANK, MLA_KV_RANK + MLA_ROPE_DIM, FOX_OUT, FOX_KV_HEADS * FOX_HEAD_DIM,
                      FOX_KV_HEADS * FOX_HEAD_DIM, FOX_HEADS])
    w = w_in[l]
    seg = lambda i: w[:, int(offs[i]):int(offs[i + 1])]
    wdq, wdk, wdv, wmq, wmkv, wfq, wfk, wfv, wfg = [seg(i) for i in range(9)]
    wckv, wkr = wmkv[:, :MLA_KV_RANK], wmkv[:, MLA_KV_RANK:]
    w_aug = jnp.concatenate([
        wdq, _swap_halves_cols(wdq), wdk, _swap_halves_cols(wdk), wdv, wmq, wckv,
        _pad_cols(wkr, LANES), _pad_cols(_swap_halves_cols(wkr), LANES), _pad_cols(wfg, LANES),
        wfq, wfk, wfv], axis=1).astype(BF16)
    assert w_aug.shape[1] == W_AUG_COLS
    wuq3 = w_uq[l].reshape(MLA_Q_RANK, MLA_HEADS, MLA_NOPE_DIM + MLA_ROPE_DIM)
    wuq_nope = wuq3[:, :, :MLA_NOPE_DIM].reshape(MLA_Q_RANK, MLA_HEADS * MLA_NOPE_DIM)
    wuq_rope = wuq3[:, :, MLA_NOPE_DIM:].reshape(MLA_Q_RANK, MLA_HEADS * MLA_ROPE_DIM)
    wuq_aug = jnp.concatenate([wuq_nope, wuq_rope, _swap_halves_cols(wuq_rope)], axis=1).astype(BF16)
    wuk3 = w_uk[l].reshape(MLA_KV_RANK, MLA_HEADS, MLA_NOPE_DIM)
    wukbd = _block_diag(jnp.transpose(wuk3, (1, 2, 0))).astype(BF16)
    wuv3 = w_uv[l].reshape(MLA_KV_RANK, MLA_HEADS, MLA_V_DIM)
    wuvbd = _block_diag(jnp.transpose(wuv3, (1, 0, 2))).astype(BF16)
    bf = _pad_cols(b_forget[l][None, :], LANES)
    mix_in_w = (w_aug, mla_q_norm_g[l][None, :], wuq_aug, wukbd, mla_kv_norm_g[l][None, :], bf)

    head_valid = jnp.concatenate([jnp.ones((DIFF_V_DIM,), F32), jnp.zeros((DIFF_V_DIM,), F32)])
    gd = jnp.tile(jnp.concatenate([diff_subln_g[l] * (1.0 - lam_init), jnp.zeros((DIFF_V_DIM,), F32)]),
                  DIFF_HEADS)[None, :]
    bmd = jnp.kron(jnp.eye(DIFF_HEADS, dtype=F32), jnp.outer(head_valid, head_valid) / DIFF_V_DIM).astype(BF16)
    wo = w_o[l]
    wod = wo[:DIFF_OUT].reshape(DIFF_HEADS, DIFF_V_DIM, D_MODEL)
    wod = jnp.concatenate([wod, jnp.zeros_like(wod)], axis=1).reshape(2 * DIFF_OUT, D_MODEL).astype(BF16)
    wom = wo[DIFF_OUT:DIFF_OUT + MLA_OUT].astype(BF16)
    wof = wo[DIFF_OUT + MLA_OUT:].astype(BF16)
    mix_out_w = (diff_lambda[l], gd, bmd, wuvbd, mla_out_g[l][None, :], fox_out_g[l][None, :], wod, wom, wof)
    return mix_in_w, mix_out_w


def _rope_tables(pos):
    half = DIFF_QK_DIM // 2
    inv_freq = ROPE_THETA ** (-jnp.arange(half, dtype=F32) / half)
    ang = pos.astype(F32)[:, None] * inv_freq[None, :]
    cos, sin = jnp.cos(ang), jnp.sin(ang)
    return (jnp.tile(jnp.concatenate([cos, cos], axis=1), (1, 8)),
            jnp.tile(jnp.concatenate([-sin, sin], axis=1), (1, 8)))


def kernel(x_prompt, x_sample, cache_diff_k, cache_diff_v, cache_mla_ckv, cache_mla_krope, cache_fox_k, cache_fox_v, cache_fox_logf, page_table, ln_g, ln_b, ffn_w1, ffn_w3, ffn_w2, w_in, b_forget, diff_lambda, diff_subln_g, mla_q_norm_g, w_uq, mla_kv_norm_g, w_uk, w_uv, mla_out_g, fox_out_g, w_o):
    depth = ffn_w1.shape[0]
    alpha = (2 * depth) ** 0.25
    bp, seq, _ = x_prompt.shape
    bs, n_dec, _ = x_sample.shape
    n_pages = page_table.shape[1]
    past_len = n_pages * PAGE_SIZE

    w1b, w3b, w2b = ffn_w1.astype(BF16), ffn_w3.astype(BF16), ffn_w2.astype(BF16)
    cos_p, sin_p = _rope_tables(jnp.arange(seq, dtype=jnp.int32))
    cos_s, sin_s = _rope_tables(past_len + jnp.arange(n_dec, dtype=jnp.int32))
    tm_s = min(MIX_TM, bs * n_dec)
    cos_s, sin_s = jnp.tile(cos_s, (tm_s // n_dec, 1)), jnp.tile(sin_s, (tm_s // n_dec, 1))

    caches = (jnp.transpose(cache_diff_k, (0, 1, 3, 4, 2)), jnp.transpose(cache_diff_v, (0, 1, 3, 4, 2)),
              cache_mla_ckv, jnp.transpose(cache_mla_krope, (0, 1, 3, 2)),
              jnp.transpose(cache_fox_k, (0, 1, 3, 4, 2)), jnp.transpose(cache_fox_v, (0, 1, 3, 4, 2)),
              jnp.transpose(cache_fox_logf, (0, 1, 3, 2)))

    xp = x_prompt.reshape(bp * seq, D_MODEL)
    xs = x_sample.reshape(bs * n_dec, D_MODEL)
    p_rows, s_rows = [], []
    for l in range(depth):
        lam_init = 0.8 - 0.6 * math.exp(-0.3 * l)
        mix_in_w, mix_out_w = _prep_layer_weights(
            l, w_in, b_forget, mla_q_norm_g, w_uq, mla_kv_norm_g, w_uk, w_uv, diff_lambda, diff_subln_g,
            mla_out_g, fox_out_g, w_o, lam_init)
        ffn = lambda x, j: _ffn_ln(x, w1b[l, j], w3b[l, j], w2b[l, j], ln_g[l, 2 * j][None, :],
                                   ln_b[l, 2 * j][None, :], alpha)

        xp = ffn(xp, 0)
        (qd, qlat, qrope, fq, dkT, dvT, ckv, krT, fkT, fvT, lfT, kdT, vdT, kmT, kfT, vfT, cT) = _mix_in(
            xp, mix_in_w, cos_p, sin_p, transposed=True, batch=bp, seq_len=seq)
        heads_first = lambda a, h: jnp.transpose(a.reshape(bp, seq, h, a.shape[1] // h), (0, 2, 1, 3))
        tokens_first = lambda a: jnp.transpose(a, (0, 2, 1, 3)).reshape(bp * seq, a.shape[1] * a.shape[3])
        od = _flash(heads_first(qd, 2 * DIFF_HEADS), kdT, vdT, None, k_rows=DIFF_QK_DIM,
                    k_idx=lambda h: (h // 4) * 2 + h % 2, v_rows=DIFF_V_DIM, v_idx=lambda h: h // 4)
        qm = jnp.concatenate([qlat.reshape(bp, seq, MLA_HEADS, MLA_KV_RANK),
                              qrope.reshape(bp, seq, MLA_HEADS, MLA_ROPE_DIM)], axis=-1)
        om = _flash(jnp.transpose(qm, (0, 2, 1, 3)), kmT, kmT, None, k_rows=MLA_QK, k_idx=lambda h: 0,
                    v_rows=MLA_KV_RANK, v_idx=lambda h: 0)
        of = _flash(heads_first(fq, FOX_HEADS), kfT, vfT, cT, k_rows=FOX_HEAD_DIM, k_idx=lambda h: h // 2,
                    v_rows=FOX_HEAD_DIM, v_idx=lambda h: h // 2)
        xp = _mix_out(xp, tokens_first(od), tokens_first(om), tokens_first(of),
                      mix_out_w + (ln_g[l, 1][None, :], ln_b[l, 1][None, :]), alpha, lam_init)
        xp = ffn(xp, 1)
        featT = lambda a, kvh: jnp.transpose(a.reshape(bp, kvh, a.shape[1] // kvh, seq), (0, 3, 1, 2))
        p_rows.append((featT(dkT, DIFF_KV_HEADS), featT(dvT, DIFF_KV_HEADS), ckv.reshape(bp, seq, MLA_KV_RANK),
                       jnp.transpose(krT, (0, 2, 1)), featT(fkT, FOX_KV_HEADS), featT(fvT, FOX_KV_HEADS),
                       jnp.transpose(lfT, (0, 2, 1))))

        xs = ffn(xs, 0)
        (qd, qlat, qrope, fq, dk, dv, ckv, kr, fk, fv, lf, c) = _mix_in(
            xs, mix_in_w, cos_s, sin_s, transposed=False, batch=bs, seq_len=n_dec)
        g_d = DIFF_HEADS // DIFF_KV_HEADS
        q6 = jnp.transpose(qd.reshape(bs, n_dec, DIFF_KV_HEADS, g_d, 2, DIFF_QK_DIM), (0, 2, 3, 4, 1, 5))
        qd_s = (q6[..., None, :] * jnp.eye(2, dtype=BF16)[None, None, None, :, None, :, None]).reshape(
            bs, DIFF_KV_HEADS, g_d * 2 * n_dec, 2 * DIFF_QK_DIM)
        qm_s = jnp.concatenate([qlat.reshape(bs, n_dec, MLA_HEADS, MLA_KV_RANK),
                                qrope.reshape(bs, n_dec, MLA_HEADS, MLA_ROPE_DIM)], axis=-1)
        qm_s = jnp.transpose(qm_s, (0, 2, 1, 3)).reshape(bs, MLA_HEADS * n_dec, MLA_QK)
        g_f = FOX_HEADS // FOX_KV_HEADS
        qf_s = jnp.transpose(fq.reshape(bs, n_dec, FOX_KV_HEADS, g_f, FOX_HEAD_DIM), (0, 2, 3, 1, 4)).reshape(
            bs, FOX_KV_HEADS, g_f * n_dec, FOX_HEAD_DIM)
        c4 = c[:, :FOX_HEADS].reshape(bs, n_dec, FOX_KV_HEADS, g_f)
        cqf = jnp.transpose(c4, (0, 2, 3, 1)).reshape(bs, FOX_KV_HEADS, g_f * n_dec, 1)
        ctn = jnp.transpose(c[:, :FOX_HEADS].reshape(bs, n_dec, FOX_HEADS), (0, 2, 1))
        per_seq = lambda a: a.reshape(bs, n_dec, a.shape[1])
        od, om, of = _sample_attn(l, page_table, qd_s, qm_s, qf_s, cqf,
                                  (per_seq(dk), per_seq(dv), per_seq(ckv), per_seq(kr), per_seq(fk), per_seq(fv), ctn),
                                  caches)
        od = jnp.transpose(od.reshape(bs, DIFF_KV_HEADS, g_d, 2, n_dec, DIFF_V_DIM), (0, 4, 1, 2, 3, 5)).reshape(
            bs * n_dec, 2 * DIFF_OUT)
        om = jnp.transpose(om.reshape(bs, MLA_HEADS, n_dec, MLA_KV_RANK), (0, 2, 1, 3)).reshape(
            bs * n_dec, MLA_HEADS * MLA_KV_RANK)
        of = jnp.transpose(of.reshape(bs, FOX_KV_HEADS, g_f, n_dec, FOX_HEAD_DIM), (0, 3, 1, 2, 4)).reshape(
            bs * n_dec, FOX_OUT)
        xs = _mix_out(xs, od, om, of, mix_out_w + (ln_g[l, 1][None, :], ln_b[l, 1][None, :]), alpha, lam_init)
        xs = ffn(xs, 1)
        s_rows.append((dk.reshape(bs, n_dec, DIFF_KV_HEADS, 2 * DIFF_QK_DIM),
                       dv.reshape(bs, n_dec, DIFF_KV_HEADS, DIFF_V_DIM),
                       ckv.reshape(bs, n_dec, MLA_KV_RANK), kr[:, :MLA_ROPE_DIM].reshape(bs, n_dec, MLA_ROPE_DIM),
                       fk.reshape(bs, n_dec, FOX_KV_HEADS, FOX_HEAD_DIM),
                       fv.reshape(bs, n_dec, FOX_KV_HEADS, FOX_HEAD_DIM),
                       lf[:, :FOX_HEADS].reshape(bs, n_dec, FOX_HEADS)))

    stack = lambda rows: tuple(jnp.stack([r[i] for r in rows], axis=0) for i in range(7))
    return (xp.reshape(bp, seq, D_MODEL), xs.reshape(bs, n_dec, D_MODEL)) + stack(p_rows) + stack(s_rows)
```

```python
import functools
import math

import numpy as np
import jax
import jax.numpy as jnp
from jax import lax
from jax.experimental import pallas as pl
from jax.experimental.pallas import tpu as pltpu

F32 = jnp.float32
BF16 = jnp.bfloat16

D_MODEL = 1024
DIFF_HEADS, DIFF_KV_HEADS, DIFF_QK_DIM, DIFF_V_DIM = 4, 2, 32, 64
MLA_HEADS, MLA_Q_RANK, MLA_KV_RANK, MLA_NOPE_DIM, MLA_ROPE_DIM, MLA_V_DIM = 8, 256, 128, 64, 32, 64
FOX_HEADS, FOX_KV_HEADS, FOX_HEAD_DIM = 4, 2, 64
D_FF = 2816
PAGE_SIZE = 128
ROPE_THETA = 10000.0
LN_EPS = 1e-5
RMS_EPS = 1e-6
MLA_SCALE = (MLA_NOPE_DIM + MLA_ROPE_DIM) ** -0.5
DIFF_OUT = DIFF_HEADS * DIFF_V_DIM
MLA_OUT = MLA_HEADS * MLA_V_DIM
FOX_OUT = FOX_HEADS * FOX_HEAD_DIM
MLA_QK = MLA_KV_RANK + MLA_ROPE_DIM
LOG2E = math.log2(math.e)
ONES_ROWS = 16
MLA_QK_PAD = MLA_KV_RANK + ONES_ROWS + MLA_ROPE_DIM

LANES = 128
VMEM_LIMIT_BYTES = 56 * 1024 * 1024

FFN_TM = 512
FFN_CHUNK = 256
MIX_TM = 512
ATT_T = 512
PAGES_PER_CHUNK = 32

SEG_DQ, SEG_DQ_SW, SEG_DK, SEG_DK_SW, SEG_DV = 0, 256, 512, 640, 768
SEG_MQ, SEG_CKV, SEG_KR, SEG_KR_SW, SEG_FG = 896, 1152, 1280, 1408, 1536
SEG_FQ, SEG_FK, SEG_FV, W_AUG_COLS = 1664, 1920, 2048, 2176


def _ln(y, g, b):
    mu = jnp.mean(y, axis=-1, keepdims=True)
    yc = y - mu
    var = jnp.mean(yc * yc, axis=-1, keepdims=True)
    return yc * lax.rsqrt(var + LN_EPS) * g + b


def _rms(x, g):
    return x * lax.rsqrt(jnp.mean(x * x, axis=-1, keepdims=True) + RMS_EPS) * g


def _cparams(sem):
    return pltpu.CompilerParams(dimension_semantics=sem, vmem_limit_bytes=VMEM_LIMIT_BYTES)


def _const_spec(shape):
    nd = len(shape)
    return pl.BlockSpec(shape, lambda *_: (0,) * nd, pipeline_mode=pl.Buffered(1))


def _ffn_ln_kernel(x_ref, w1_ref, w3_ref, w2_ref, g_ref, b_ref, o_ref, *, alpha):
    x = x_ref[...]
    xb = x.astype(BF16)
    acc = jnp.zeros(x.shape, F32)
    for c in range(D_FF // FFN_CHUNK):
        sl = slice(c * FFN_CHUNK, (c + 1) * FFN_CHUNK)
        h1 = jnp.dot(xb, w1_ref[:, sl], preferred_element_type=F32)
        h3 = jnp.dot(xb, w3_ref[:, sl], preferred_element_type=F32)
        h = (h1 * jax.nn.sigmoid(h1) * h3).astype(BF16)
        acc = acc + jnp.dot(h, w2_ref[sl, :], preferred_element_type=F32)
    o_ref[...] = _ln(alpha * x + 0.5 * acc, g_ref[...], b_ref[...])


def _ffn_ln(x, w1, w3, w2, g, b, alpha):
    n = x.shape[0]
    tm = min(FFN_TM, n)
    return pl.pallas_call(
        functools.partial(_ffn_ln_kernel, alpha=alpha),
        grid=(n // tm,),
        in_specs=[
            pl.BlockSpec((tm, D_MODEL), lambda i: (i, 0)),
            _const_spec((D_MODEL, D_FF)), _const_spec((D_MODEL, D_FF)), _const_spec((D_FF, D_MODEL)),
            _const_spec((1, D_MODEL)), _const_spec((1, D_MODEL)),
        ],
        out_specs=pl.BlockSpec((tm, D_MODEL), lambda i: (i, 0)),
        out_shape=jax.ShapeDtypeStruct((n, D_MODEL), F32),
        compiler_params=_cparams(("parallel",)),
        name="ffn_ln",
    )(x, w1, w3, w2, g, b)


def _log_sigmoid(z):
    return jnp.minimum(z, 0.0) - jnp.log1p(jnp.exp(-jnp.abs(z)))


def _mix_in_kernel(x_ref, w_ref, cos_ref, sin_ref, gq_ref, wuq_ref, wukbd_ref, gkv_ref, bf_ref, *refs,
                   transposed, seq_len, tiles_per_seq):
    if transposed:
        (qd_ref, qm_ref, fq_ref, dkT_ref, dvT_ref, ckv_ref, krT_ref, fkT_ref, fvT_ref,
         lfT_ref, kdT_ref, vdT_ref, kmT_ref, kfT_ref, vfT_ref, cT_ref, carry_ref) = refs
    else:
        (qd_ref, qlat_ref, qrope_ref, fq_ref, dk_ref, dv_ref, ckv_ref, kr_ref, fk_ref, fv_ref,
         lf_ref, c_ref) = refs
    tm = x_ref.shape[0]
    xb = x_ref[...].astype(BF16)
    proj = jnp.dot(xb, w_ref[...], preferred_element_type=F32)
    cos = cos_ref[...]
    sin = sin_ref[...]
    cos1, sin1 = cos[:, :LANES], sin[:, :LANES]

    dq = proj[:, SEG_DQ:SEG_DQ + 256] * cos + proj[:, SEG_DQ_SW:SEG_DQ_SW + 256] * sin
    dk = proj[:, SEG_DK:SEG_DK + 128] * cos1 + proj[:, SEG_DK_SW:SEG_DK_SW + 128] * sin1
    dv = proj[:, SEG_DV:SEG_DV + 128]
    kr = proj[:, SEG_KR:SEG_KR + 128] * cos1 + proj[:, SEG_KR_SW:SEG_KR_SW + 128] * sin1
    fk = proj[:, SEG_FK:SEG_FK + 128]
    fv = proj[:, SEG_FV:SEG_FV + 128]

    mqn = _rms(proj[:, SEG_MQ:SEG_MQ + MLA_Q_RANK], gq_ref[...]).astype(BF16)
    q2 = jnp.dot(mqn, wuq_ref[...], preferred_element_type=F32)
    qrope = q2[:, 512:768] * cos + q2[:, 768:1024] * sin
    qlat = jnp.dot(q2[:, :512].astype(BF16), wukbd_ref[...], preferred_element_type=F32)
    ckv = _rms(proj[:, SEG_CKV:SEG_CKV + MLA_KV_RANK], gkv_ref[...])

    lane = lax.broadcasted_iota(jnp.int32, (tm, LANES), 1)
    lf = jnp.where(lane < FOX_HEADS, _log_sigmoid(proj[:, SEG_FG:SEG_FG + 128] + bf_ref[...]), 0.0)

    qd = (dq * (DIFF_QK_DIM ** -0.5 * LOG2E)).astype(BF16)
    qlat = (qlat * (MLA_SCALE * LOG2E)).astype(BF16)
    qrope = (qrope * (MLA_SCALE * LOG2E)).astype(BF16)
    fq = (proj[:, SEG_FQ:SEG_FQ + 256] * (FOX_HEAD_DIM ** -0.5 * LOG2E)).astype(BF16)
    ckv_ref[...] = ckv

    if transposed:
        for h in range(2 * DIFF_HEADS):
            qd_ref[h] = qd[:, h * DIFF_QK_DIM:(h + 1) * DIFF_QK_DIM]
        for h in range(FOX_HEADS):
            fq_ref[h] = fq[:, h * FOX_HEAD_DIM:(h + 1) * FOX_HEAD_DIM]
        zpad = jnp.zeros((tm, ONES_ROWS), BF16)
        for h in range(MLA_HEADS):
            qm_ref[h] = jnp.concatenate(
                [qlat[:, h * MLA_KV_RANK:(h + 1) * MLA_KV_RANK], zpad,
                 qrope[:, h * MLA_ROPE_DIM:(h + 1) * MLA_ROPE_DIM]], axis=1)

        dkT, dvT, fkT, fvT = dk.T, dv.T, fk.T, fv.T
        krT = kr.T[:MLA_ROPE_DIM]
        lfT = lf.T[:8]
        dkT_ref[...] = dkT
        dvT_ref[...] = dvT
        fkT_ref[...] = fkT
        fvT_ref[...] = fvT
        krT_ref[...] = krT
        lfT_ref[...] = lfT[:FOX_HEADS]
        ones_rows = jnp.where(lax.broadcasted_iota(jnp.int32, (ONES_ROWS, tm), 0) == 0, 1.0, 0.0).astype(BF16)
        kdT_ref[...] = dkT.astype(BF16)
        kfT_ref[...] = fkT.astype(BF16)
        for src, dst, width in ((dvT, vdT_ref, DIFF_V_DIM), (fvT, vfT_ref, FOX_HEAD_DIM)):
            for h in range(src.shape[0] // width):
                base = h * (width + ONES_ROWS)
                dst[base:base + width, :] = src[h * width:(h + 1) * width].astype(BF16)
                dst[base + width:base + width + ONES_ROWS, :] = ones_rows
        kmT_ref[0:MLA_KV_RANK, :] = ckv.T.astype(BF16)
        kmT_ref[MLA_KV_RANK:MLA_KV_RANK + ONES_ROWS, :] = ones_rows
        kmT_ref[MLA_KV_RANK + ONES_ROWS:MLA_QK_PAD, :] = krT.astype(BF16)

        @pl.when(pl.program_id(0) % tiles_per_seq == 0)
        def _():
            carry_ref[...] = jnp.zeros_like(carry_ref)
        lane8 = lax.broadcasted_iota(jnp.int32, (8, LANES), 1)
        carry = carry_ref[:, 0:1]
        for p in range(tm // LANES):
            piece = lfT[:, p * LANES:(p + 1) * LANES]
            total = jnp.sum(piece, axis=1, keepdims=True)
            k = 1
            while k < LANES:
                piece = piece + jnp.where(lane8 >= k, pltpu.roll(piece, k, 1), 0.0)
                k *= 2
            piece = piece + carry
            for h in range(FOX_HEADS):
                cT_ref[h, :, p * LANES:(p + 1) * LANES] = piece[h:h + 1, :]
            carry = carry + total
        carry_ref[...] = jnp.broadcast_to(carry, carry_ref.shape)
    else:
        qd_ref[...] = qd
        qlat_ref[...] = qlat
        qrope_ref[...] = qrope
        fq_ref[...] = fq
        dk_ref[...] = dk
        dv_ref[...] = dv
        fk_ref[...] = fk
        fv_ref[...] = fv
        kr_ref[...] = kr
        lf_ref[...] = lf
        row = lax.broadcasted_iota(jnp.int32, (tm, LANES), 0) % seq_len
        c = lf
        k = 1
        while k < seq_len:
            c = c + jnp.where(row >= k, pltpu.roll(c, k, 0), 0.0)
            k *= 2
        c_ref[...] = c


def _mix_in(x, wts, cos_t, sin_t, *, transposed, batch, seq_len):
    n = x.shape[0]
    tm = min(MIX_TM, n)
    nt = n // tm
    tok = lambda w: pl.BlockSpec((tm, w), lambda i: (i, 0))
    tok_out = lambda w, dt: jax.ShapeDtypeStruct((n, w), dt)
    if transposed:
        tps = seq_len // tm
        tbl = pl.BlockSpec((tm, 256), lambda i: (i % tps, 0))
        featT = lambda r: pl.BlockSpec((None, r, tm), lambda i: (i // tps, 0, i % tps))
        featT_out = lambda r: jax.ShapeDtypeStruct((batch, r, seq_len), F32)
        blkT = lambda r: pl.BlockSpec((None, None, r, tm), lambda i: (i // tps, i % tps, 0, 0))
        blkT_out = lambda r: jax.ShapeDtypeStruct((batch, tps, r, tm), BF16)
        headq = lambda h, d: pl.BlockSpec((None, h, tm, d), lambda i: (i // tps, 0, i % tps, 0))
        headq_out = lambda h, d: jax.ShapeDtypeStruct((batch, h, seq_len, d), BF16)
        v_rows = lambda kvh, d: kvh * (d + ONES_ROWS)
        out_specs = [headq(2 * DIFF_HEADS, DIFF_QK_DIM), headq(MLA_HEADS, MLA_QK_PAD), headq(FOX_HEADS, FOX_HEAD_DIM),
                     featT(128), featT(128), tok(128), featT(MLA_ROPE_DIM), featT(128), featT(128),
                     featT(FOX_HEADS),
                     blkT(128), blkT(v_rows(DIFF_KV_HEADS, DIFF_V_DIM)), blkT(MLA_QK_PAD), blkT(128),
                     blkT(v_rows(FOX_KV_HEADS, FOX_HEAD_DIM)),
                     pl.BlockSpec((None, FOX_HEADS, None, 1, tm), lambda i: (i // tps, 0, i % tps, 0, 0))]
        out_shape = [headq_out(2 * DIFF_HEADS, DIFF_QK_DIM), headq_out(MLA_HEADS, MLA_QK_PAD),
                     headq_out(FOX_HEADS, FOX_HEAD_DIM),
                     featT_out(128), featT_out(128), tok_out(128, F32), featT_out(MLA_ROPE_DIM),
                     featT_out(128), featT_out(128), featT_out(FOX_HEADS),
                     blkT_out(128), blkT_out(v_rows(DIFF_KV_HEADS, DIFF_V_DIM)), blkT_out(MLA_QK_PAD), blkT_out(128),
                     blkT_out(v_rows(FOX_KV_HEADS, FOX_HEAD_DIM)),
                     jax.ShapeDtypeStruct((batch, FOX_HEADS, tps, 1, tm), F32)]
        scratch = [pltpu.VMEM((8, LANES), F32)]
        sem = ("arbitrary",)
    else:
        tps = 1
        tbl = pl.BlockSpec((tm, 256), lambda i: (0, 0))
        out_specs = [tok(256), tok(1024), tok(256), tok(256)] + [tok(128)] * 8
        out_shape = [tok_out(256, BF16), tok_out(1024, BF16), tok_out(256, BF16), tok_out(256, BF16)] + \
                    [tok_out(128, F32)] * 8
        scratch = []
        sem = ("parallel",)
    w_aug, gq, wuq, wukbd, gkv, bf = wts
    return pl.pallas_call(
        functools.partial(_mix_in_kernel, transposed=transposed, seq_len=seq_len, tiles_per_seq=tps),
        grid=(nt,),
        in_specs=[tok(D_MODEL), _const_spec(w_aug.shape), tbl, tbl, _const_spec(gq.shape),
                  _const_spec(wuq.shape), _const_spec(wukbd.shape), _const_spec(gkv.shape),
                  _const_spec(bf.shape)],
        out_specs=out_specs,
        out_shape=out_shape,
        scratch_shapes=scratch,
        compiler_params=_cparams(sem),
        name="mix_in_prompt" if transposed else "mix_in_sample",
    )(x, w_aug, cos_t, sin_t, gq, wuq, wukbd, gkv, bf)


def _flash_kernel(q_ref, kT_ref, vT_ref, *refs, has_bias, split_keys, dv):
    if has_bias:
        cT_ref, o_ref = refs
    else:
        (o_ref,) = refs
    i = pl.program_id(2)
    n_sub, t, _ = q_ref.shape
    k_rows = kT_ref.shape[1] // n_sub if split_keys else kT_ref.shape[1]
    qs = [q_ref[g] for g in range(n_sub)]
    row = lax.broadcasted_iota(jnp.int32, (t, t), 0)
    col = lax.broadcasted_iota(jnp.int32, (t, t), 1)
    if has_bias:
        cq = [LOG2E * jnp.sum(jnp.where(row == col, cT_ref[g, i], 0.0), axis=1, keepdims=True)
              for g in range(n_sub)]

    def step(j, carry, masked):
        kT = kT_ref[j]
        vT = vT_ref[j]
        out = []
        for g in range(n_sub):
            m, acc = carry[g]
            kg = kT[g * k_rows:(g + 1) * k_rows] if split_keys else kT
            s = jnp.dot(qs[g], kg, preferred_element_type=F32)
            if has_bias:
                s = s + (cq[g] - LOG2E * cT_ref[g, j])
            if masked:
                s = jnp.where(col <= row, s, -jnp.inf)
            m_new = jnp.maximum(m, jnp.max(s, axis=1, keepdims=True))
            p = jnp.exp2(s - m_new).astype(BF16)
            pv = lax.dot_general(p, vT, (((1,), (1,)), ((), ())), preferred_element_type=F32)
            out.append((m_new, jnp.exp2(m - m_new) * acc + pv))
        return tuple(out)

    init = tuple((jnp.full((t, 1), -jnp.inf, F32), jnp.zeros((t, dv + ONES_ROWS), F32)) for _ in range(n_sub))
    carry = lax.fori_loop(0, i, functools.partial(step, masked=False), init)
    res = step(i, carry, True)
    outs = [acc[:, :dv] / acc[:, dv:dv + 1] for _, acc in res]
    o_ref[...] = outs[0] if n_sub == 1 else jnp.concatenate(outs, axis=1)


def _flash(q, kT, vT, cT, *, n_sub, split_keys, k_rows, k_idx, dv, v_idx, name):
    b, hn, seq, dq = q.shape
    nkb, t = kT.shape[1], kT.shape[3]
    hg_n = hn // n_sub
    nq = seq // t
    in_specs = [
        pl.BlockSpec((None, n_sub, t, dq), lambda bi, h, i: (bi, h, i, 0)),
        pl.BlockSpec((None, nkb, k_rows, t), lambda bi, h, i: (bi, 0, k_idx(h), 0)),
        pl.BlockSpec((None, nkb, dv + ONES_ROWS, t), lambda bi, h, i: (bi, 0, v_idx(h), 0)),
    ]
    args = [q, kT, vT]
    if cT is not None:
        in_specs.append(pl.BlockSpec((None, n_sub, nkb, 1, t), lambda bi, h, i: (bi, h, 0, 0, 0)))
        args.append(cT)
    return pl.pallas_call(
        functools.partial(_flash_kernel, has_bias=cT is not None, split_keys=split_keys, dv=dv),
        grid=(b, hg_n, nq),
        in_specs=in_specs,
        out_specs=pl.BlockSpec((t, n_sub * dv), lambda bi, h, i: (bi * nq + i, h)),
        out_shape=jax.ShapeDtypeStruct((b * seq, hn * dv), F32),
        compiler_params=_cparams(("parallel", "parallel", "parallel")),
        name=name,
    )(*args)


def _softmax_update(s, state, v_dot):
    m, l, acc = state
    m_new = jnp.maximum(m, jnp.max(s, axis=1, keepdims=True))
    a = jnp.exp2(m - m_new)
    p = jnp.exp2(s - m_new)
    return m_new, a * l + jnp.sum(p, axis=1, keepdims=True), a * acc + v_dot(p.astype(BF16))


def _nt(a, b):
    return lax.dot_general(a, b, (((1,), (1,)), ((), ())), preferred_element_type=F32)


def _sample_attn_kernel(pt_ref,
                        qd_ref, qm_ref, qf_ref, cqf_ref,
                        dkn_ref, dvn_ref, ckvn_ref, krn_ref, fkn_ref, fvn_ref, ctn_ref,
                        dkc, dvc, ckvc, krc, fkc, fvc, lfc,
                        od_ref, om_ref, of_ref,
                        *scratch, layer, n_chunks, ppc):
    n_arrays = 7
    caches = (dkc, dvc, ckvc, krc, fkc, fvc, lfc)
    bufs = (scratch[:n_arrays], scratch[n_arrays:2 * n_arrays])
    sems = scratch[2 * n_arrays]
    b = pl.program_id(0)
    nb = pl.num_programs(0)
    n_dec = ctn_ref.shape[1]
    n_pairs = n_chunks // 2

    def page_dst(a, buf, p):
        rows = pl.ds(p * PAGE_SIZE, PAGE_SIZE)
        if a == 2:
            return buf.at[rows, :]
        return buf.at[:, rows] if buf.ndim == 2 else buf.at[:, :, rows]

    def start_chunk(bi, ci, slot):
        for p in range(ppc):
            page = pt_ref[bi, ci * ppc + p]
            for a in range(n_arrays):
                pltpu.make_async_copy(caches[a].at[layer, page], page_dst(a, bufs[slot][a], p),
                                      sems.at[slot, a]).start()

    def wait_chunk(slot):
        for a in range(n_arrays):
            buf = bufs[slot][a]
            pltpu.make_async_copy(buf, buf, sems.at[slot, a]).wait()

    @pl.when(b == 0)
    def _():
        start_chunk(0, n_chunks - 1, 0)

    qd = [qd_ref[h] for h in range(DIFF_KV_HEADS)]
    qm = qm_ref[...]
    qf = [qf_ref[h] for h in range(FOX_KV_HEADS)]
    cqf = [cqf_ref[h] for h in range(FOX_KV_HEADS)]
    g_fox = FOX_HEADS // FOX_KV_HEADS
    lane4 = lax.broadcasted_iota(jnp.int32, (FOX_HEADS, LANES), 1)

    def init(rows, dv):
        return (jnp.full((rows, 1), -jnp.inf, F32), jnp.zeros((rows, 1), F32), jnp.zeros((rows, dv), F32))

    def fox_bias(h, d_rows):
        return jnp.concatenate(
            [cqf[h][g * n_dec:(g + 1) * n_dec] + d_rows[h * g_fox + g:h * g_fox + g + 1, :] for g in range(g_fox)],
            axis=0) * LOG2E

    def compute_chunk(slot, carry):
        dk_buf, dv_buf, ckv_buf, kr_buf, fk_buf, fv_buf, lf_buf = bufs[slot]
        sd, sm, sf, lf_after = carry
        new_sd = []
        for h in range(DIFF_KV_HEADS):
            s = jnp.dot(qd[h], dk_buf[h].astype(BF16), preferred_element_type=F32)
            v = dv_buf[h].astype(BF16)
            new_sd.append(_softmax_update(s, sd[h], lambda p, v=v: _nt(p, v)))
        ckv = ckv_buf[...].astype(BF16)
        s = _nt(qm[:, :MLA_KV_RANK], ckv) + jnp.dot(qm[:, MLA_KV_RANK:], kr_buf[...].astype(BF16),
                                                   preferred_element_type=F32)
        new_sm = _softmax_update(s, sm, lambda p: jnp.dot(p, ckv, preferred_element_type=F32))
        pieces = [None] * ppc
        for p in reversed(range(ppc)):
            x = lf_buf[:, p * PAGE_SIZE:(p + 1) * PAGE_SIZE]
            total = jnp.sum(x, axis=1, keepdims=True)
            sfx = x
            kk = 1
            while kk < PAGE_SIZE:
                sfx = sfx + jnp.where(lane4 + kk < PAGE_SIZE, pltpu.roll(sfx, PAGE_SIZE - kk, 1), 0.0)
                kk *= 2
            pieces[p] = sfx - x + lf_after
            lf_after = lf_after + total
        d_rows = jnp.concatenate(pieces, axis=1)
        new_sf = []
        for h in range(FOX_KV_HEADS):
            s = jnp.dot(qf[h], fk_buf[h].astype(BF16), preferred_element_type=F32) + fox_bias(h, d_rows)
            v = fv_buf[h].astype(BF16)
            new_sf.append(_softmax_update(s, sf[h], lambda p, v=v: _nt(p, v)))
        return tuple(new_sd), new_sm, tuple(new_sf), lf_after

    def pair_body(j, carry):
        c0 = n_chunks - 1 - 2 * j
        wait_chunk(0)
        start_chunk(b, c0 - 1, 1)
        carry = compute_chunk(0, carry)
        wait_chunk(1)
        last = j + 1 == n_pairs
        start_chunk(jnp.where(last, jnp.where(b + 1 == nb, 0, b + 1), b),
                    jnp.where(last, n_chunks - 1, c0 - 2), 0)
        return compute_chunk(1, carry)

    rows_d = qd_ref.shape[1]
    rows_f = qf_ref.shape[1]
    carry0 = (tuple(init(rows_d, DIFF_V_DIM) for _ in range(DIFF_KV_HEADS)),
              init(qm_ref.shape[0], MLA_KV_RANK),
              tuple(init(rows_f, FOX_HEAD_DIM) for _ in range(FOX_KV_HEADS)),
              jnp.zeros((FOX_HEADS, 1), F32))
    sd, sm, sf, _ = lax.fori_loop(0, n_pairs, pair_body, carry0)

    @pl.when(b == nb - 1)
    def _():
        wait_chunk(0)

    def causal(rows):
        tq = lax.broadcasted_iota(jnp.int32, (rows, n_dec), 0) % n_dec
        tk = lax.broadcasted_iota(jnp.int32, (rows, n_dec), 1)
        return tk <= tq

    for h in range(DIFF_KV_HEADS):
        kn = dkn_ref[:, h * 64:(h + 1) * 64].astype(BF16)
        vn = dvn_ref[:, h * 64:(h + 1) * 64].astype(BF16)
        s = jnp.where(causal(rows_d), _nt(qd[h], kn), -jnp.inf)
        _, l, acc = _softmax_update(s, sd[h], lambda p, vn=vn: jnp.dot(p, vn, preferred_element_type=F32))
        od_ref[h] = acc / l
    ckvn = ckvn_ref[...].astype(BF16)
    s = _nt(qm[:, :MLA_KV_RANK], ckvn) + _nt(qm[:, MLA_KV_RANK:], krn_ref[:, :MLA_ROPE_DIM].astype(BF16))
    s = jnp.where(causal(qm_ref.shape[0]), s, -jnp.inf)
    _, l, acc = _softmax_update(s, sm, lambda p: jnp.dot(p, ckvn, preferred_element_type=F32))
    om_ref[...] = acc / l
    ctn = ctn_ref[...]
    for h in range(FOX_KV_HEADS):
        kn = fkn_ref[:, h * 64:(h + 1) * 64].astype(BF16)
        vn = fvn_ref[:, h * 64:(h + 1) * 64].astype(BF16)
        s = _nt(qf[h], kn) + fox_bias(h, -ctn)
        s = jnp.where(causal(rows_f), s, -jnp.inf)
        _, l, acc = _softmax_update(s, sf[h], lambda p, vn=vn: jnp.dot(p, vn, preferred_element_type=F32))
        of_ref[h] = acc / l


def _sample_attn(layer, page_table, qd, qm, qf, cqf, new_rows, caches):
    nb, n_pages = page_table.shape
    ppc = min(PAGES_PER_CHUNK, n_pages)
    n_chunks = n_pages // ppc
    ck = ppc * PAGE_SIZE
    n_dec = new_rows[0].shape[1]
    per_b = lambda shp: pl.BlockSpec((None,) + shp, lambda b, pt: (b,) + (0,) * len(shp))
    any_spec = pl.BlockSpec(memory_space=pl.ANY)
    rows_d, rows_m, rows_f = qd.shape[2], qm.shape[1], qf.shape[2]
    in_specs = [per_b(qd.shape[1:]), per_b(qm.shape[1:]), per_b(qf.shape[1:]), per_b(cqf.shape[1:])]
    in_specs += [per_b(a.shape[1:]) for a in new_rows]
    in_specs += [any_spec] * 7
    out_shape = [jax.ShapeDtypeStruct((nb, DIFF_KV_HEADS, rows_d, DIFF_V_DIM), F32),
                 jax.ShapeDtypeStruct((nb, rows_m, MLA_KV_RANK), F32),
                 jax.ShapeDtypeStruct((nb, FOX_KV_HEADS, rows_f, FOX_HEAD_DIM), F32)]
    out_specs = [per_b(s.shape[1:]) for s in out_shape]
    assert n_chunks % 2 == 0, "the chunk loop handles two chunks (one per buffer slot) per trip"
    slot_bufs = [
        pltpu.VMEM((DIFF_KV_HEADS, 64, ck), F32), pltpu.VMEM((DIFF_KV_HEADS, 64, ck), F32),
        pltpu.VMEM((ck, MLA_KV_RANK), F32), pltpu.VMEM((MLA_ROPE_DIM, ck), F32),
        pltpu.VMEM((FOX_KV_HEADS, 64, ck), F32), pltpu.VMEM((FOX_KV_HEADS, 64, ck), F32),
        pltpu.VMEM((FOX_HEADS, ck), F32),
    ]
    scratch = slot_bufs + slot_bufs + [pltpu.SemaphoreType.DMA((2, len(slot_bufs)))]
    return pl.pallas_call(
        functools.partial(_sample_attn_kernel, layer=layer, n_chunks=n_chunks, ppc=ppc),
        grid_spec=pltpu.PrefetchScalarGridSpec(
            num_scalar_prefetch=1, grid=(nb,), in_specs=in_specs, out_specs=out_specs,
            scratch_shapes=scratch),
        out_shape=out_shape,
        compiler_params=_cparams(("arbitrary",)),
        name="sample_attn",
    )(page_table, qd, qm, qf, cqf, *new_rows, *caches)


def _mix_out_kernel(x_ref, od_ref, om_ref, of_ref, dl_ref, gd_ref, bmd_ref, wuv_ref, gm_ref, gf_ref,
                    wod_ref, wom_ref, wof_ref, g_ref, b_ref, o_ref, *, alpha, lam_init):
    dl = dl_ref[...]
    lam = (jnp.exp(jnp.sum(dl[0:1] * dl[1:2], axis=1, keepdims=True))
           - jnp.exp(jnp.sum(dl[2:3] * dl[3:4], axis=1, keepdims=True)) + lam_init)
    od = od_ref[...]
    d = od - lam * pltpu.roll(od, od.shape[1] - DIFF_V_DIM, 1)
    ms = jnp.dot((d * d).astype(BF16), bmd_ref[...], preferred_element_type=F32)
    yd = d * lax.rsqrt(ms + RMS_EPS) * gd_ref[...]
    ov = jnp.dot(om_ref[...].astype(BF16), wuv_ref[...], preferred_element_type=F32)
    ym = _rms(ov, gm_ref[...])
    yf = _rms(of_ref[...], gf_ref[...])
    y = (jnp.dot(yd.astype(BF16), wod_ref[...], preferred_element_type=F32)
         + jnp.dot(ym.astype(BF16), wom_ref[...], preferred_element_type=F32)
         + jnp.dot(yf.astype(BF16), wof_ref[...], preferred_element_type=F32))
    o_ref[...] = _ln(alpha * x_ref[...] + y, g_ref[...], b_ref[...])


def _mix_out(x, od, om, of, wts, alpha, lam_init):
    n = x.shape[0]
    tm = min(MIX_TM, n)
    tok = lambda w: pl.BlockSpec((tm, w), lambda i: (i, 0))
    return pl.pallas_call(
        functools.partial(_mix_out_kernel, alpha=alpha, lam_init=lam_init),
        grid=(n // tm,),
        in_specs=[tok(D_MODEL), tok(od.shape[1]), tok(om.shape[1]), tok(of.shape[1])]
                 + [_const_spec(w.shape) for w in wts],
        out_specs=tok(D_MODEL),
        out_shape=jax.ShapeDtypeStruct((n, D_MODEL), F32),
        compiler_params=_cparams(("parallel",)),
        name="mix_out",
    )(x, od, om, of, *wts)


def _swap_halves_cols(w, chunk=32):
    c = w.shape[-1]
    w4 = w.reshape(w.shape[:-1] + (c // chunk, 2, chunk // 2))
    return jnp.flip(w4, axis=-2).reshape(w.shape)


def _pad_cols(w, width):
    return jnp.pad(w, [(0, 0)] * (w.ndim - 1) + [(0, width - w.shape[-1])])


def _block_diag(blocks):
    h, r, c = blocks.shape
    eye = jnp.eye(h, dtype=blocks.dtype)
    return (blocks[:, :, None, :] * eye[:, None, :, None]).reshape(h * r, h * c)


def _prep_layer_weights(l, w_in, b_forget, mla_q_norm_g, w_uq, mla_kv_norm_g, w_uk, w_uv, diff_lambda,
                        diff_subln_g, mla_out_g, fox_out_g, w_o, lam_init):
    offs = np.cumsum([0, DIFF_HEADS * 2 * DIFF_QK_DIM, DIFF_KV_HEADS * 2 * DIFF_QK_DIM, DIFF_KV_HEADS * DIFF_V_DIM,
                      MLA_Q_RANK, MLA_KV_RANK + MLA_ROPE_DIM, FOX_OUT, FOX_KV_HEADS * FOX_HEAD_DIM,
                      FOX_KV_HEADS * FOX_HEAD_DIM, FOX_HEADS])
    w = w_in[l]
    seg = lambda i: w[:, int(offs[i]):int(offs[i + 1])]
    wdq, wdk, wdv, wmq, wmkv, wfq, wfk, wfv, wfg = [seg(i) for i in range(9)]
    wckv, wkr = wmkv[:, :MLA_KV_RANK], wmkv[:, MLA_KV_RANK:]
    w_aug = jnp.concatenate([
        wdq, _swap_halves_cols(wdq), wdk, _swap_halves_cols(wdk), wdv, wmq, wckv,
        _pad_cols(wkr, LANES), _pad_cols(_swap_halves_cols(wkr), LANES), _pad_cols(wfg, LANES),
        wfq, wfk, wfv], axis=1).astype(BF16)
    assert w_aug.shape[1] == W_AUG_COLS
    wuq3 = w_uq[l].reshape(MLA_Q_RANK, MLA_HEADS, MLA_NOPE_DIM + MLA_ROPE_DIM)
    wuq_nope = wuq3[:, :, :MLA_NOPE_DIM].reshape(MLA_Q_RANK, MLA_HEADS * MLA_NOPE_DIM)
    wuq_rope = wuq3[:, :, MLA_NOPE_DIM:].reshape(MLA_Q_RANK, MLA_HEADS * MLA_ROPE_DIM)
    wuq_aug = jnp.concatenate([wuq_nope, wuq_rope, _swap_halves_cols(wuq_rope)], axis=1).astype(BF16)
    wuk3 = w_uk[l].reshape(MLA_KV_RANK, MLA_HEADS, MLA_NOPE_DIM)
    wukbd = _block_diag(jnp.transpose(wuk3, (1, 2, 0))).astype(BF16)
    wuv3 = w_uv[l].reshape(MLA_KV_RANK, MLA_HEADS, MLA_V_DIM)
    wuvbd = _block_diag(jnp.transpose(wuv3, (1, 0, 2))).astype(BF16)
    bf = _pad_cols(b_forget[l][None, :], LANES)
    mix_in_w = (w_aug, mla_q_norm_g[l][None, :], wuq_aug, wukbd, mla_kv_norm_g[l][None, :], bf)

    head_valid = jnp.concatenate([jnp.ones((DIFF_V_DIM,), F32), jnp.zeros((DIFF_V_DIM,), F32)])
    gd = jnp.tile(jnp.concatenate([diff_subln_g[l] * (1.0 - lam_init), jnp.zeros((DIFF_V_DIM,), F32)]),
                  DIFF_HEADS)[None, :]
    bmd = jnp.kron(jnp.eye(DIFF_HEADS, dtype=F32), jnp.outer(head_valid, head_valid) / DIFF_V_DIM).astype(BF16)
    wo = w_o[l]
    wod = wo[:DIFF_OUT].reshape(DIFF_HEADS, DIFF_V_DIM, D_MODEL)
    wod = jnp.concatenate([wod, jnp.zeros_like(wod)], axis=1).reshape(2 * DIFF_OUT, D_MODEL).astype(BF16)
    wom = wo[DIFF_OUT:DIFF_OUT + MLA_OUT].astype(BF16)
    wof = wo[DIFF_OUT + MLA_OUT:].astype(BF16)
    mix_out_w = (diff_lambda[l], gd, bmd, wuvbd, mla_out_g[l][None, :], fox_out_g[l][None, :], wod, wom, wof)
    return mix_in_w, mix_out_w


def _rope_tables(pos):
    half = DIFF_QK_DIM // 2
    inv_freq = ROPE_THETA ** (-jnp.arange(half, dtype=F32) / half)
    ang = pos.astype(F32)[:, None] * inv_freq[None, :]
    cos, sin = jnp.cos(ang), jnp.sin(ang)
    return (jnp.tile(jnp.concatenate([cos, cos], axis=1), (1, 8)),
            jnp.tile(jnp.concatenate([-sin, sin], axis=1), (1, 8)))


def kernel(x_prompt, x_sample, cache_diff_k, cache_diff_v, cache_mla_ckv, cache_mla_krope, cache_fox_k, cache_fox_v, cache_fox_logf, page_table, ln_g, ln_b, ffn_w1, ffn_w3, ffn_w2, w_in, b_forget, diff_lambda, diff_subln_g, mla_q_norm_g, w_uq, mla_kv_norm_g, w_uk, w_uv, mla_out_g, fox_out_g, w_o):
    depth = ffn_w1.shape[0]
    alpha = (2 * depth) ** 0.25
    bp, seq, _ = x_prompt.shape
    bs, n_dec, _ = x_sample.shape
    n_pages = page_table.shape[1]
    past_len = n_pages * PAGE_SIZE

    w1b, w3b, w2b = ffn_w1.astype(BF16), ffn_w3.astype(BF16), ffn_w2.astype(BF16)
    cos_p, sin_p = _rope_tables(jnp.arange(seq, dtype=jnp.int32))
    cos_s, sin_s = _rope_tables(past_len + jnp.arange(n_dec, dtype=jnp.int32))
    tm_s = min(MIX_TM, bs * n_dec)
    cos_s, sin_s = jnp.tile(cos_s, (tm_s // n_dec, 1)), jnp.tile(sin_s, (tm_s // n_dec, 1))

    caches = (jnp.transpose(cache_diff_k, (0, 1, 3, 4, 2)), jnp.transpose(cache_diff_v, (0, 1, 3, 4, 2)),
              cache_mla_ckv, jnp.transpose(cache_mla_krope, (0, 1, 3, 2)),
              jnp.transpose(cache_fox_k, (0, 1, 3, 4, 2)), jnp.transpose(cache_fox_v, (0, 1, 3, 4, 2)),
              jnp.transpose(cache_fox_logf, (0, 1, 3, 2)))

    xp = x_prompt.reshape(bp * seq, D_MODEL)
    xs = x_sample.reshape(bs * n_dec, D_MODEL)
    p_rows, s_rows = [], []
    for l in range(depth):
        lam_init = 0.8 - 0.6 * math.exp(-0.3 * l)
        mix_in_w, mix_out_w = _prep_layer_weights(
            l, w_in, b_forget, mla_q_norm_g, w_uq, mla_kv_norm_g, w_uk, w_uv, diff_lambda, diff_subln_g,
            mla_out_g, fox_out_g, w_o, lam_init)
        ffn = lambda x, j: _ffn_ln(x, w1b[l, j], w3b[l, j], w2b[l, j], ln_g[l, 2 * j][None, :],
                                   ln_b[l, 2 * j][None, :], alpha)

        xp = ffn(xp, 0)
        (qd, qm, fq, dkT, dvT, ckv, krT, fkT, fvT, lfT, kdT, vdT, kmT, kfT, vfT, cT) = _mix_in(
            xp, mix_in_w, cos_p, sin_p, transposed=True, batch=bp, seq_len=seq)
        od = _flash(qd, kdT, vdT, None, n_sub=2, split_keys=True, k_rows=2 * DIFF_QK_DIM,
                    k_idx=lambda h: h // (DIFF_HEADS // DIFF_KV_HEADS), dv=DIFF_V_DIM,
                    v_idx=lambda h: h // (DIFF_HEADS // DIFF_KV_HEADS), name="flash_diff")
        om = _flash(qm, kmT, kmT, None, n_sub=1, split_keys=False, k_rows=MLA_QK_PAD, k_idx=lambda h: 0,
                    dv=MLA_KV_RANK, v_idx=lambda h: 0, name="flash_mla")
        of = _flash(fq, kfT, vfT, cT, n_sub=FOX_HEADS // FOX_KV_HEADS, split_keys=False, k_rows=FOX_HEAD_DIM,
                    k_idx=lambda h: h, dv=FOX_HEAD_DIM, v_idx=lambda h: h, name="flash_fox")
        xp = _mix_out(xp, od, om, of,
                      mix_out_w + (ln_g[l, 1][None, :], ln_b[l, 1][None, :]), alpha, lam_init)
        xp = ffn(xp, 1)
        featT = lambda a, kvh: jnp.transpose(a.reshape(bp, kvh, a.shape[1] // kvh, seq), (0, 3, 1, 2))
        p_rows.append((featT(dkT, DIFF_KV_HEADS), featT(dvT, DIFF_KV_HEADS), ckv.reshape(bp, seq, MLA_KV_RANK),
                       jnp.transpose(krT, (0, 2, 1)), featT(fkT, FOX_KV_HEADS), featT(fvT, FOX_KV_HEADS),
                       jnp.transpose(lfT, (0, 2, 1))))

        xs = ffn(xs, 0)
        (qd, qlat, qrope, fq, dk, dv, ckv, kr, fk, fv, lf, c) = _mix_in(
            xs, mix_in_w, cos_s, sin_s, transposed=False, batch=bs, seq_len=n_dec)
        g_d = DIFF_HEADS // DIFF_KV_HEADS
        q6 = jnp.transpose(qd.reshape(bs, n_dec, DIFF_KV_HEADS, g_d, 2, DIFF_QK_DIM), (0, 2, 3, 4, 1, 5))
        qd_s = (q6[..., None, :] * jnp.eye(2, dtype=BF16)[None, None, None, :, None, :, None]).reshape(
            bs, DIFF_KV_HEADS, g_d * 2 * n_dec, 2 * DIFF_QK_DIM)
        qm_s = jnp.concatenate([qlat.reshape(bs, n_dec, MLA_HEADS, MLA_KV_RANK),
                                qrope.reshape(bs, n_dec, MLA_HEADS, MLA_ROPE_DIM)], axis=-1)
        qm_s = jnp.transpose(qm_s, (0, 2, 1, 3)).reshape(bs, MLA_HEADS * n_dec, MLA_QK)
        g_f = FOX_HEADS // FOX_KV_HEADS
        qf_s = jnp.transpose(fq.reshape(bs, n_dec, FOX_KV_HEADS, g_f, FOX_HEAD_DIM), (0, 2, 3, 1, 4)).reshape(
            bs, FOX_KV_HEADS, g_f * n_dec, FOX_HEAD_DIM)
        c4 = c[:, :FOX_HEADS].reshape(bs, n_dec, FOX_KV_HEADS, g_f)
        cqf = jnp.transpose(c4, (0, 2, 3, 1)).reshape(bs, FOX_KV_HEADS, g_f * n_dec, 1)
        ctn = jnp.transpose(c[:, :FOX_HEADS].reshape(bs, n_dec, FOX_HEADS), (0, 2, 1))
        per_seq = lambda a: a.reshape(bs, n_dec, a.shape[1])
        od, om, of = _sample_attn(l, page_table, qd_s, qm_s, qf_s, cqf,
                                  (per_seq(dk), per_seq(dv), per_seq(ckv), per_seq(kr), per_seq(fk), per_seq(fv), ctn),
                                  caches)
        od = jnp.transpose(od.reshape(bs, DIFF_KV_HEADS, g_d, 2, n_dec, DIFF_V_DIM), (0, 4, 1, 2, 3, 5)).reshape(
            bs * n_dec, 2 * DIFF_OUT)
        om = jnp.transpose(om.reshape(bs, MLA_HEADS, n_dec, MLA_KV_RANK), (0, 2, 1, 3)).reshape(
            bs * n_dec, MLA_HEADS * MLA_KV_RANK)
        of = jnp.transpose(of.reshape(bs, FOX_KV_HEADS, g_f, n_dec, FOX_HEAD_DIM), (0, 3, 1, 2, 4)).reshape(
            bs * n_dec, FOX_OUT)
        xs = _mix_out(xs, od, om, of, mix_out_w + (ln_g[l, 1][None, :], ln_b[l, 1][None, :]), alpha, lam_init)
        xs = ffn(xs, 1)
        s_rows.append((dk.reshape(bs, n_dec, DIFF_KV_HEADS, 2 * DIFF_QK_DIM),
                       dv.reshape(bs, n_dec, DIFF_KV_HEADS, DIFF_V_DIM),
                       ckv.reshape(bs, n_dec, MLA_KV_RANK), kr[:, :MLA_ROPE_DIM].reshape(bs, n_dec, MLA_ROPE_DIM),
                       fk.reshape(bs, n_dec, FOX_KV_HEADS, FOX_HEAD_DIM),
                       fv.reshape(bs, n_dec, FOX_KV_HEADS, FOX_HEAD_DIM),
                       lf[:, :FOX_HEADS].reshape(bs, n_dec, FOX_HEADS)))

    stack = lambda rows: tuple(jnp.stack([r[i] for r in rows], axis=0) for i in range(7))
    return (xp.reshape(bp, seq, D_MODEL), xs.reshape(bs, n_dec, D_MODEL)) + stack(p_rows) + stack(s_rows)
```

```python
import functools
import math

import numpy as np
import jax
import jax.numpy as jnp
from jax import lax
from jax.experimental import pallas as pl
from jax.experimental.pallas import tpu as pltpu

F32 = jnp.float32
BF16 = jnp.bfloat16

D_MODEL = 1024
DIFF_HEADS, DIFF_KV_HEADS, DIFF_QK_DIM, DIFF_V_DIM = 4, 2, 32, 64
MLA_HEADS, MLA_Q_RANK, MLA_KV_RANK, MLA_NOPE_DIM, MLA_ROPE_DIM, MLA_V_DIM = 8, 256, 128, 64, 32, 64
FOX_HEADS, FOX_KV_HEADS, FOX_HEAD_DIM = 4, 2, 64
D_FF = 2816
PAGE_SIZE = 128
ROPE_THETA = 10000.0
LN_EPS = 1e-5
RMS_EPS = 1e-6
MLA_SCALE = (MLA_NOPE_DIM + MLA_ROPE_DIM) ** -0.5
DIFF_OUT = DIFF_HEADS * DIFF_V_DIM
MLA_OUT = MLA_HEADS * MLA_V_DIM
FOX_OUT = FOX_HEADS * FOX_HEAD_DIM
MLA_QK = MLA_KV_RANK + MLA_ROPE_DIM
LOG2E = math.log2(math.e)
ONES_ROWS = 16
MLA_QK_PAD = MLA_KV_RANK + ONES_ROWS + MLA_ROPE_DIM

LANES = 128
VMEM_LIMIT_BYTES = 56 * 1024 * 1024

FFN_TM = 512
FFN_CHUNK = 256
MIX_TM = 512
ATT_T = 1024
PAGES_PER_CHUNK = 64

SEG_DQ, SEG_DQ_SW, SEG_DK, SEG_DK_SW, SEG_DV = 0, 256, 512, 640, 768
SEG_MQ, SEG_CKV, SEG_KR, SEG_KR_SW, SEG_FG = 896, 1152, 1280, 1408, 1536
SEG_FQ, SEG_FK, SEG_FV, W_AUG_COLS = 1664, 1920, 2048, 2176


def _ln(y, g, b):
    mu = jnp.mean(y, axis=-1, keepdims=True)
    yc = y - mu
    var = jnp.mean(yc * yc, axis=-1, keepdims=True)
    return yc * lax.rsqrt(var + LN_EPS) * g + b


def _rms(x, g):
    return x * lax.rsqrt(jnp.mean(x * x, axis=-1, keepdims=True) + RMS_EPS) * g


def _cparams(sem):
    return pltpu.CompilerParams(dimension_semantics=sem, vmem_limit_bytes=VMEM_LIMIT_BYTES)


def _const_spec(shape):
    nd = len(shape)
    return pl.BlockSpec(shape, lambda *_: (0,) * nd, pipeline_mode=pl.Buffered(1))


def _ffn_ln_kernel(x_ref, w1_ref, w3_ref, w2_ref, g_ref, b_ref, o_ref, *, alpha):
    x = x_ref[...]
    xb = x.astype(BF16)
    acc = jnp.zeros(x.shape, F32)
    for c in range(D_FF // FFN_CHUNK):
        sl = slice(c * FFN_CHUNK, (c + 1) * FFN_CHUNK)
        h1 = jnp.dot(xb, w1_ref[:, sl], preferred_element_type=F32)
        h3 = jnp.dot(xb, w3_ref[:, sl], preferred_element_type=F32)
        h = (h1 * jax.nn.sigmoid(h1) * h3).astype(BF16)
        acc = acc + jnp.dot(h, w2_ref[sl, :], preferred_element_type=F32)
    o_ref[...] = _ln(alpha * x + 0.5 * acc, g_ref[...], b_ref[...])


def _ffn_ln(x, w1, w3, w2, g, b, alpha):
    n = x.shape[0]
    tm = min(FFN_TM, n)
    return pl.pallas_call(
        functools.partial(_ffn_ln_kernel, alpha=alpha),
        grid=(n // tm,),
        in_specs=[
            pl.BlockSpec((tm, D_MODEL), lambda i: (i, 0)),
            _const_spec((D_MODEL, D_FF)), _const_spec((D_MODEL, D_FF)), _const_spec((D_FF, D_MODEL)),
            _const_spec((1, D_MODEL)), _const_spec((1, D_MODEL)),
        ],
        out_specs=pl.BlockSpec((tm, D_MODEL), lambda i: (i, 0)),
        out_shape=jax.ShapeDtypeStruct((n, D_MODEL), F32),
        compiler_params=_cparams(("parallel",)),
        name="ffn_ln",
    )(x, w1, w3, w2, g, b)


def _log_sigmoid(z):
    return jnp.minimum(z, 0.0) - jnp.log1p(jnp.exp(-jnp.abs(z)))


def _mix_in_kernel(x_ref, w_ref, cos_ref, sin_ref, gq_ref, wuq_ref, wukbd_ref, gkv_ref, bf_ref, *refs,
                   transposed, seq_len, tiles_per_seq):
    if transposed:
        (qd_ref, qm_ref, fq_ref, dkT_ref, dvT_ref, ckv_ref, krT_ref, fkT_ref, fvT_ref,
         lfT_ref, kdT_ref, vdT_ref, kmT_ref, kfT_ref, vfT_ref, cT_ref, carry_ref) = refs
    else:
        (qd_ref, qlat_ref, qrope_ref, fq_ref, dk_ref, dv_ref, ckv_ref, kr_ref, fk_ref, fv_ref,
         lf_ref, c_ref) = refs
    tm = x_ref.shape[0]
    xb = x_ref[...].astype(BF16)
    proj = jnp.dot(xb, w_ref[...], preferred_element_type=F32)
    cos = cos_ref[...]
    sin = sin_ref[...]
    cos1, sin1 = cos[:, :LANES], sin[:, :LANES]

    dq = proj[:, SEG_DQ:SEG_DQ + 256] * cos + proj[:, SEG_DQ_SW:SEG_DQ_SW + 256] * sin
    dk = proj[:, SEG_DK:SEG_DK + 128] * cos1 + proj[:, SEG_DK_SW:SEG_DK_SW + 128] * sin1
    dv = proj[:, SEG_DV:SEG_DV + 128]
    kr = proj[:, SEG_KR:SEG_KR + 128] * cos1 + proj[:, SEG_KR_SW:SEG_KR_SW + 128] * sin1
    fk = proj[:, SEG_FK:SEG_FK + 128]
    fv = proj[:, SEG_FV:SEG_FV + 128]

    mqn = _rms(proj[:, SEG_MQ:SEG_MQ + MLA_Q_RANK], gq_ref[...]).astype(BF16)
    q2 = jnp.dot(mqn, wuq_ref[...], preferred_element_type=F32)
    qrope = q2[:, 512:768] * cos + q2[:, 768:1024] * sin
    qlat = jnp.dot(q2[:, :512].astype(BF16), wukbd_ref[...], preferred_element_type=F32)
    ckv = _rms(proj[:, SEG_CKV:SEG_CKV + MLA_KV_RANK], gkv_ref[...])

    lane = lax.broadcasted_iota(jnp.int32, (tm, LANES), 1)
    lf = jnp.where(lane < FOX_HEADS, _log_sigmoid(proj[:, SEG_FG:SEG_FG + 128] + bf_ref[...]), 0.0)

    qd = (dq * (DIFF_QK_DIM ** -0.5 * LOG2E)).astype(BF16)
    qlat = (qlat * (MLA_SCALE * LOG2E)).astype(BF16)
    qrope = (qrope * (MLA_SCALE * LOG2E)).astype(BF16)
    fq = (proj[:, SEG_FQ:SEG_FQ + 256] * (FOX_HEAD_DIM ** -0.5 * LOG2E)).astype(BF16)
    ckv_ref[...] = ckv

    if transposed:
        for h in range(2 * DIFF_HEADS):
            qd_ref[h] = qd[:, h * DIFF_QK_DIM:(h + 1) * DIFF_QK_DIM]
        for h in range(FOX_HEADS):
            fq_ref[h] = fq[:, h * FOX_HEAD_DIM:(h + 1) * FOX_HEAD_DIM]
        zpad = jnp.zeros((tm, ONES_ROWS), BF16)
        for h in range(MLA_HEADS):
            qm_ref[h] = jnp.concatenate(
                [qlat[:, h * MLA_KV_RANK:(h + 1) * MLA_KV_RANK], zpad,
                 qrope[:, h * MLA_ROPE_DIM:(h + 1) * MLA_ROPE_DIM]], axis=1)

        dkT, dvT, fkT, fvT = dk.T, dv.T, fk.T, fv.T
        krT = kr.T[:MLA_ROPE_DIM]
        lfT = lf.T[:8]
        dkT_ref[...] = dkT
        dvT_ref[...] = dvT
        fkT_ref[...] = fkT
        fvT_ref[...] = fvT
        krT_ref[...] = krT
        lfT_ref[...] = lfT[:FOX_HEADS]
        ones_rows = jnp.where(lax.broadcasted_iota(jnp.int32, (ONES_ROWS, tm), 0) == 0, 1.0, 0.0).astype(BF16)
        kdT_ref[...] = dkT.astype(BF16)
        kfT_ref[...] = fkT.astype(BF16)
        for src, dst, width in ((dvT, vdT_ref, DIFF_V_DIM), (fvT, vfT_ref, FOX_HEAD_DIM)):
            for h in range(src.shape[0] // width):
                base = h * (width + ONES_ROWS)
                dst[base:base + width, :] = src[h * width:(h + 1) * width].astype(BF16)
                dst[base + width:base + width + ONES_ROWS, :] = ones_rows
        kmT_ref[0:MLA_KV_RANK, :] = ckv.T.astype(BF16)
        kmT_ref[MLA_KV_RANK:MLA_KV_RANK + ONES_ROWS, :] = ones_rows
        kmT_ref[MLA_KV_RANK + ONES_ROWS:MLA_QK_PAD, :] = krT.astype(BF16)

        @pl.when(pl.program_id(0) % tiles_per_seq == 0)
        def _():
            carry_ref[...] = jnp.zeros_like(carry_ref)
        lane8 = lax.broadcasted_iota(jnp.int32, (8, LANES), 1)
        carry = carry_ref[:, 0:1]
        for p in range(tm // LANES):
            piece = lfT[:, p * LANES:(p + 1) * LANES]
            total = jnp.sum(piece, axis=1, keepdims=True)
            k = 1
            while k < LANES:
                piece = piece + jnp.where(lane8 >= k, pltpu.roll(piece, k, 1), 0.0)
                k *= 2
            piece = piece + carry
            for h in range(FOX_HEADS):
                cT_ref[h, :, p * LANES:(p + 1) * LANES] = piece[h:h + 1, :]
            carry = carry + total
        carry_ref[...] = jnp.broadcast_to(carry, carry_ref.shape)
    else:
        qd_ref[...] = qd
        qlat_ref[...] = qlat
        qrope_ref[...] = qrope
        fq_ref[...] = fq
        dk_ref[...] = dk
        dv_ref[...] = dv
        fk_ref[...] = fk
        fv_ref[...] = fv
        kr_ref[...] = kr
        lf_ref[...] = lf
        row = lax.broadcasted_iota(jnp.int32, (tm, LANES), 0) % seq_len
        c = lf
        k = 1
        while k < seq_len:
            c = c + jnp.where(row >= k, pltpu.roll(c, k, 0), 0.0)
            k *= 2
        c_ref[...] = c


def _mix_in(x, wts, cos_t, sin_t, *, transposed, batch, seq_len):
    n = x.shape[0]
    tm = min(MIX_TM, n)
    nt = n // tm
    tok = lambda w: pl.BlockSpec((tm, w), lambda i: (i, 0))
    tok_out = lambda w, dt: jax.ShapeDtypeStruct((n, w), dt)
    if transposed:
        tps = seq_len // tm
        tbl = pl.BlockSpec((tm, 256), lambda i: (i % tps, 0))
        featT = lambda r: pl.BlockSpec((None, r, tm), lambda i: (i // tps, 0, i % tps))
        featT_out = lambda r: jax.ShapeDtypeStruct((batch, r, seq_len), F32)
        kb = min(ATT_T, seq_len) // tm
        blkT = lambda r: pl.BlockSpec((None, None, r, tm), lambda i: (i // tps, (i % tps) // kb, 0, (i % tps) % kb))
        blkT_out = lambda r: jax.ShapeDtypeStruct((batch, tps // kb, r, kb * tm), BF16)
        headq = lambda h, d: pl.BlockSpec((None, h, tm, d), lambda i: (i // tps, 0, i % tps, 0))
        headq_out = lambda h, d: jax.ShapeDtypeStruct((batch, h, seq_len, d), BF16)
        v_rows = lambda kvh, d: kvh * (d + ONES_ROWS)
        out_specs = [headq(2 * DIFF_HEADS, DIFF_QK_DIM), headq(MLA_HEADS, MLA_QK_PAD), headq(FOX_HEADS, FOX_HEAD_DIM),
                     featT(128), featT(128), tok(128), featT(MLA_ROPE_DIM), featT(128), featT(128),
                     featT(FOX_HEADS),
                     blkT(128), blkT(v_rows(DIFF_KV_HEADS, DIFF_V_DIM)), blkT(MLA_QK_PAD), blkT(128),
                     blkT(v_rows(FOX_KV_HEADS, FOX_HEAD_DIM)),
                     pl.BlockSpec((None, FOX_HEADS, None, 1, tm),
                                  lambda i: (i // tps, 0, (i % tps) // kb, 0, (i % tps) % kb))]
        out_shape = [headq_out(2 * DIFF_HEADS, DIFF_QK_DIM), headq_out(MLA_HEADS, MLA_QK_PAD),
                     headq_out(FOX_HEADS, FOX_HEAD_DIM),
                     featT_out(128), featT_out(128), tok_out(128, F32), featT_out(MLA_ROPE_DIM),
                     featT_out(128), featT_out(128), featT_out(FOX_HEADS),
                     blkT_out(128), blkT_out(v_rows(DIFF_KV_HEADS, DIFF_V_DIM)), blkT_out(MLA_QK_PAD), blkT_out(128),
                     blkT_out(v_rows(FOX_KV_HEADS, FOX_HEAD_DIM)),
                     jax.ShapeDtypeStruct((batch, FOX_HEADS, tps // kb, 1, kb * tm), F32)]
        scratch = [pltpu.VMEM((8, LANES), F32)]
        sem = ("arbitrary",)
    else:
        tps = 1
        tbl = pl.BlockSpec((tm, 256), lambda i: (0, 0))
        out_specs = [tok(256), tok(1024), tok(256), tok(256)] + [tok(128)] * 8
        out_shape = [tok_out(256, BF16), tok_out(1024, BF16), tok_out(256, BF16), tok_out(256, BF16)] + \
                    [tok_out(128, F32)] * 8
        scratch = []
        sem = ("parallel",)
    w_aug, gq, wuq, wukbd, gkv, bf = wts
    return pl.pallas_call(
        functools.partial(_mix_in_kernel, transposed=transposed, seq_len=seq_len, tiles_per_seq=tps),
        grid=(nt,),
        in_specs=[tok(D_MODEL), _const_spec(w_aug.shape), tbl, tbl, _const_spec(gq.shape),
                  _const_spec(wuq.shape), _const_spec(wukbd.shape), _const_spec(gkv.shape),
                  _const_spec(bf.shape)],
        out_specs=out_specs,
        out_shape=out_shape,
        scratch_shapes=scratch,
        compiler_params=_cparams(sem),
        name="mix_in_prompt" if transposed else "mix_in_sample",
    )(x, w_aug, cos_t, sin_t, gq, wuq, wukbd, gkv, bf)


def _flash_kernel(q_ref, kT_ref, vT_ref, *refs, has_bias, split_keys, dv):
    if has_bias:
        cT_ref, o_ref = refs
    else:
        (o_ref,) = refs
    i = pl.program_id(2)
    n_sub, t, _ = q_ref.shape
    k_rows = kT_ref.shape[1] // n_sub if split_keys else kT_ref.shape[1]
    qs = [q_ref[g] for g in range(n_sub)]
    row = lax.broadcasted_iota(jnp.int32, (t, t), 0)
    col = lax.broadcasted_iota(jnp.int32, (t, t), 1)
    if has_bias:
        cq = [LOG2E * jnp.sum(jnp.where(row == col, cT_ref[g, i], 0.0), axis=1, keepdims=True)
              for g in range(n_sub)]

    def step(j, carry, masked):
        kT = kT_ref[j]
        vT = vT_ref[j]
        out = []
        for g in range(n_sub):
            m, acc = carry[g]
            kg = kT[g * k_rows:(g + 1) * k_rows] if split_keys else kT
            s = jnp.dot(qs[g], kg, preferred_element_type=F32)
            if has_bias:
                s = s + (cq[g] - LOG2E * cT_ref[g, j])
            if masked:
                s = jnp.where(col <= row, s, -jnp.inf)
            m_new = jnp.maximum(m, jnp.max(s, axis=1, keepdims=True))
            p = jnp.exp2(s - m_new).astype(BF16)
            pv = lax.dot_general(p, vT, (((1,), (1,)), ((), ())), preferred_element_type=F32)
            out.append((m_new, jnp.exp2(m - m_new) * acc + pv))
        return tuple(out)

    init = tuple((jnp.full((t, 1), -jnp.inf, F32), jnp.zeros((t, dv + ONES_ROWS), F32)) for _ in range(n_sub))
    carry = lax.fori_loop(0, i, functools.partial(step, masked=False), init)
    res = step(i, carry, True)
    outs = [acc[:, :dv] / acc[:, dv:dv + 1] for _, acc in res]
    o_ref[...] = outs[0] if n_sub == 1 else jnp.concatenate(outs, axis=1)


def _flash(q, kT, vT, cT, *, n_sub, split_keys, k_rows, k_idx, dv, v_idx, name):
    b, hn, seq, dq = q.shape
    nkb, t = kT.shape[1], kT.shape[3]
    hg_n = hn // n_sub
    nq = seq // t
    in_specs = [
        pl.BlockSpec((None, n_sub, t, dq), lambda bi, h, i: (bi, h, i, 0)),
        pl.BlockSpec((None, nkb, k_rows, t), lambda bi, h, i: (bi, 0, k_idx(h), 0)),
        pl.BlockSpec((None, nkb, dv + ONES_ROWS, t), lambda bi, h, i: (bi, 0, v_idx(h), 0)),
    ]
    args = [q, kT, vT]
    if cT is not None:
        in_specs.append(pl.BlockSpec((None, n_sub, nkb, 1, t), lambda bi, h, i: (bi, h, 0, 0, 0)))
        args.append(cT)
    return pl.pallas_call(
        functools.partial(_flash_kernel, has_bias=cT is not None, split_keys=split_keys, dv=dv),
        grid=(b, hg_n, nq),
        in_specs=in_specs,
        out_specs=pl.BlockSpec((t, n_sub * dv), lambda bi, h, i: (bi * nq + i, h)),
        out_shape=jax.ShapeDtypeStruct((b * seq, hn * dv), F32),
        compiler_params=_cparams(("parallel", "parallel", "parallel")),
        name=name,
    )(*args)


def _softmax_update(s, state, v_dot):
    m, l, acc = state
    m_new = jnp.maximum(m, jnp.max(s, axis=1, keepdims=True))
    a = jnp.exp2(m - m_new)
    p = jnp.exp2(s - m_new)
    return m_new, a * l + jnp.sum(p, axis=1, keepdims=True), a * acc + v_dot(p.astype(BF16))


def _nt(a, b):
    return lax.dot_general(a, b, (((1,), (1,)), ((), ())), preferred_element_type=F32)


def _sample_attn_kernel(pt_ref,
                        qd_ref, qm_ref, qf_ref, cqf_ref,
                        dkn_ref, dvn_ref, ckvn_ref, krn_ref, fkn_ref, fvn_ref, ctn_ref,
                        dkc, dvc, ckvc, krc, fkc, fvc, lfc,
                        od_ref, om_ref, of_ref,
                        *scratch, layer, n_chunks, ppc):
    n_arrays = 7
    caches = (dkc, dvc, ckvc, krc, fkc, fvc, lfc)
    bufs = (scratch[:n_arrays], scratch[n_arrays:2 * n_arrays])
    sems = scratch[2 * n_arrays]
    b = pl.program_id(0)
    nb = pl.num_programs(0)
    n_dec = ctn_ref.shape[1]
    n_pairs = n_chunks // 2

    def page_dst(a, buf, p):
        rows = pl.ds(p * PAGE_SIZE, PAGE_SIZE)
        if a == 2:
            return buf.at[rows, :]
        return buf.at[:, rows] if buf.ndim == 2 else buf.at[:, :, rows]

    def start_chunk(bi, ci, slot):
        for p in range(ppc):
            page = pt_ref[bi, ci * ppc + p]
            for a in range(n_arrays):
                pltpu.make_async_copy(caches[a].at[layer, page], page_dst(a, bufs[slot][a], p),
                                      sems.at[slot, a]).start()

    def wait_chunk(slot):
        for a in range(n_arrays):
            buf = bufs[slot][a]
            pltpu.make_async_copy(buf, buf, sems.at[slot, a]).wait()

    @pl.when(b == 0)
    def _():
        start_chunk(0, n_chunks - 1, 0)

    qd = [qd_ref[h] for h in range(DIFF_KV_HEADS)]
    qm = qm_ref[...]
    qf = [qf_ref[h] for h in range(FOX_KV_HEADS)]
    cqf = [cqf_ref[h] for h in range(FOX_KV_HEADS)]
    g_fox = FOX_HEADS // FOX_KV_HEADS
    lane4 = lax.broadcasted_iota(jnp.int32, (FOX_HEADS, LANES), 1)

    def init(rows, dv):
        return (jnp.full((rows, 1), -jnp.inf, F32), jnp.zeros((rows, 1), F32), jnp.zeros((rows, dv), F32))

    def fox_bias(h, d_rows):
        return jnp.concatenate(
            [cqf[h][g * n_dec:(g + 1) * n_dec] + d_rows[h * g_fox + g:h * g_fox + g + 1, :] for g in range(g_fox)],
            axis=0) * LOG2E

    def compute_chunk(slot, carry):
        dk_buf, dv_buf, ckv_buf, kr_buf, fk_buf, fv_buf, lf_buf = bufs[slot]
        sd, sm, sf, lf_after = carry
        new_sd = []
        for h in range(DIFF_KV_HEADS):
            s = jnp.dot(qd[h], dk_buf[h].astype(BF16), preferred_element_type=F32)
            v = dv_buf[h].astype(BF16)
            new_sd.append(_softmax_update(s, sd[h], lambda p, v=v: _nt(p, v)))
        ckv = ckv_buf[...].astype(BF16)
        s = _nt(qm[:, :MLA_KV_RANK], ckv) + jnp.dot(qm[:, MLA_KV_RANK:], kr_buf[...].astype(BF16),
                                                   preferred_element_type=F32)
        new_sm = _softmax_update(s, sm, lambda p: jnp.dot(p, ckv, preferred_element_type=F32))
        pieces = [None] * ppc
        for p in reversed(range(ppc)):
            x = lf_buf[:, p * PAGE_SIZE:(p + 1) * PAGE_SIZE]
            total = jnp.sum(x, axis=1, keepdims=True)
            sfx = x
            kk = 1
            while kk < PAGE_SIZE:
                sfx = sfx + jnp.where(lane4 + kk < PAGE_SIZE, pltpu.roll(sfx, PAGE_SIZE - kk, 1), 0.0)
                kk *= 2
            pieces[p] = sfx - x + lf_after
            lf_after = lf_after + total
        d_rows = jnp.concatenate(pieces, axis=1)
        new_sf = []
        for h in range(FOX_KV_HEADS):
            s = jnp.dot(qf[h], fk_buf[h].astype(BF16), preferred_element_type=F32) + fox_bias(h, d_rows)
            v = fv_buf[h].astype(BF16)
            new_sf.append(_softmax_update(s, sf[h], lambda p, v=v: _nt(p, v)))
        return tuple(new_sd), new_sm, tuple(new_sf), lf_after

    def start_chunk_first(bi, ci, slot):
        @pl.when(b >= 0)
        def _():
            start_chunk(bi, ci, slot)

    def pair_body(j, carry):
        c0 = n_chunks - 1 - 2 * j
        wait_chunk(0)
        start_chunk_first(b, c0 - 1, 1)
        carry = compute_chunk(0, carry)
        wait_chunk(1)
        last = j + 1 == n_pairs
        start_chunk_first(jnp.where(last, jnp.where(b + 1 == nb, 0, b + 1), b),
                          jnp.where(last, n_chunks - 1, c0 - 2), 0)
        return compute_chunk(1, carry)

    rows_d = qd_ref.shape[1]
    rows_f = qf_ref.shape[1]
    carry0 = (tuple(init(rows_d, DIFF_V_DIM) for _ in range(DIFF_KV_HEADS)),
              init(qm_ref.shape[0], MLA_KV_RANK),
              tuple(init(rows_f, FOX_HEAD_DIM) for _ in range(FOX_KV_HEADS)),
              jnp.zeros((FOX_HEADS, 1), F32))
    sd, sm, sf, _ = lax.fori_loop(0, n_pairs, pair_body, carry0)

    @pl.when(b == nb - 1)
    def _():
        wait_chunk(0)

    def causal(rows):
        tq = lax.broadcasted_iota(jnp.int32, (rows, n_dec), 0) % n_dec
        tk = lax.broadcasted_iota(jnp.int32, (rows, n_dec), 1)
        return tk <= tq

    for h in range(DIFF_KV_HEADS):
        kn = dkn_ref[:, h * 64:(h + 1) * 64].astype(BF16)
        vn = dvn_ref[:, h * 64:(h + 1) * 64].astype(BF16)
        s = jnp.where(causal(rows_d), _nt(qd[h], kn), -jnp.inf)
        _, l, acc = _softmax_update(s, sd[h], lambda p, vn=vn: jnp.dot(p, vn, preferred_element_type=F32))
        od_ref[h] = acc / l
    ckvn = ckvn_ref[...].astype(BF16)
    s = _nt(qm[:, :MLA_KV_RANK], ckvn) + _nt(qm[:, MLA_KV_RANK:], krn_ref[:, :MLA_ROPE_DIM].astype(BF16))
    s = jnp.where(causal(qm_ref.shape[0]), s, -jnp.inf)
    _, l, acc = _softmax_update(s, sm, lambda p: jnp.dot(p, ckvn, preferred_element_type=F32))
    om_ref[...] = acc / l
    ctn = ctn_ref[...]
    for h in range(FOX_KV_HEADS):
        kn = fkn_ref[:, h * 64:(h + 1) * 64].astype(BF16)
        vn = fvn_ref[:, h * 64:(h + 1) * 64].astype(BF16)
        s = _nt(qf[h], kn) + fox_bias(h, -ctn)
        s = jnp.where(causal(rows_f), s, -jnp.inf)
        _, l, acc = _softmax_update(s, sf[h], lambda p, vn=vn: jnp.dot(p, vn, preferred_element_type=F32))
        of_ref[h] = acc / l


def _sample_attn(layer, page_table, qd, qm, qf, cqf, new_rows, caches):
    nb, n_pages = page_table.shape
    ppc = min(PAGES_PER_CHUNK, n_pages)
    n_chunks = n_pages // ppc
    ck = ppc * PAGE_SIZE
    n_dec = new_rows[0].shape[1]
    per_b = lambda shp: pl.BlockSpec((None,) + shp, lambda b, pt: (b,) + (0,) * len(shp))
    any_spec = pl.BlockSpec(memory_space=pl.ANY)
    rows_d, rows_m, rows_f = qd.shape[2], qm.shape[1], qf.shape[2]
    in_specs = [per_b(qd.shape[1:]), per_b(qm.shape[1:]), per_b(qf.shape[1:]), per_b(cqf.shape[1:])]
    in_specs += [per_b(a.shape[1:]) for a in new_rows]
    in_specs += [any_spec] * 7
    out_shape = [jax.ShapeDtypeStruct((nb, DIFF_KV_HEADS, rows_d, DIFF_V_DIM), F32),
                 jax.ShapeDtypeStruct((nb, rows_m, MLA_KV_RANK), F32),
                 jax.ShapeDtypeStruct((nb, FOX_KV_HEADS, rows_f, FOX_HEAD_DIM), F32)]
    out_specs = [per_b(s.shape[1:]) for s in out_shape]
    assert n_chunks % 2 == 0, "the chunk loop handles two chunks (one per buffer slot) per trip"
    slot_bufs = [
        pltpu.VMEM((DIFF_KV_HEADS, 64, ck), F32), pltpu.VMEM((DIFF_KV_HEADS, 64, ck), F32),
        pltpu.VMEM((ck, MLA_KV_RANK), F32), pltpu.VMEM((MLA_ROPE_DIM, ck), F32),
        pltpu.VMEM((FOX_KV_HEADS, 64, ck), F32), pltpu.VMEM((FOX_KV_HEADS, 64, ck), F32),
        pltpu.VMEM((FOX_HEADS, ck), F32),
    ]
    scratch = slot_bufs + slot_bufs + [pltpu.SemaphoreType.DMA((2, len(slot_bufs)))]
    return pl.pallas_call(
        functools.partial(_sample_attn_kernel, layer=layer, n_chunks=n_chunks, ppc=ppc),
        grid_spec=pltpu.PrefetchScalarGridSpec(
            num_scalar_prefetch=1, grid=(nb,), in_specs=in_specs, out_specs=out_specs,
            scratch_shapes=scratch),
        out_shape=out_shape,
        compiler_params=_cparams(("arbitrary",)),
        name="sample_attn",
    )(page_table, qd, qm, qf, cqf, *new_rows, *caches)


def _mix_out_kernel(x_ref, od_ref, om_ref, of_ref, dl_ref, gd_ref, bmd_ref, wuv_ref, gm_ref, gf_ref,
                    wod_ref, wom_ref, wof_ref, g_ref, b_ref, o_ref, *, alpha, lam_init):
    dl = dl_ref[...]
    lam = (jnp.exp(jnp.sum(dl[0:1] * dl[1:2], axis=1, keepdims=True))
           - jnp.exp(jnp.sum(dl[2:3] * dl[3:4], axis=1, keepdims=True)) + lam_init)
    od = od_ref[...]
    d = od - lam * pltpu.roll(od, od.shape[1] - DIFF_V_DIM, 1)
    ms = jnp.dot((d * d).astype(BF16), bmd_ref[...], preferred_element_type=F32)
    yd = d * lax.rsqrt(ms + RMS_EPS) * gd_ref[...]
    ov = jnp.dot(om_ref[...].astype(BF16), wuv_ref[...], preferred_element_type=F32)
    ym = _rms(ov, gm_ref[...])
    yf = _rms(of_ref[...], gf_ref[...])
    y = (jnp.dot(yd.astype(BF16), wod_ref[...], preferred_element_type=F32)
         + jnp.dot(ym.astype(BF16), wom_ref[...], preferred_element_type=F32)
         + jnp.dot(yf.astype(BF16), wof_ref[...], preferred_element_type=F32))
    o_ref[...] = _ln(alpha * x_ref[...] + y, g_ref[...], b_ref[...])


def _mix_out(x, od, om, of, wts, alpha, lam_init):
    n = x.shape[0]
    tm = min(MIX_TM, n)
    tok = lambda w: pl.BlockSpec((tm, w), lambda i: (i, 0))
    return pl.pallas_call(
        functools.partial(_mix_out_kernel, alpha=alpha, lam_init=lam_init),
        grid=(n // tm,),
        in_specs=[tok(D_MODEL), tok(od.shape[1]), tok(om.shape[1]), tok(of.shape[1])]
                 + [_const_spec(w.shape) for w in wts],
        out_specs=tok(D_MODEL),
        out_shape=jax.ShapeDtypeStruct((n, D_MODEL), F32),
        compiler_params=_cparams(("parallel",)),
        name="mix_out",
    )(x, od, om, of, *wts)


def _swap_halves_cols(w, chunk=32):
    c = w.shape[-1]
    w4 = w.reshape(w.shape[:-1] + (c // chunk, 2, chunk // 2))
    return jnp.flip(w4, axis=-2).reshape(w.shape)


def _pad_cols(w, width):
    return jnp.pad(w, [(0, 0)] * (w.ndim - 1) + [(0, width - w.shape[-1])])


def _block_diag(blocks):
    h, r, c = blocks.shape
    eye = jnp.eye(h, dtype=blocks.dtype)
    return (blocks[:, :, None, :] * eye[:, None, :, None]).reshape(h * r, h * c)


def _prep_layer_weights(l, w_in, b_forget, mla_q_norm_g, w_uq, mla_kv_norm_g, w_uk, w_uv, diff_lambda,
                        diff_subln_g, mla_out_g, fox_out_g, w_o, lam_init):
    offs = np.cumsum([0, DIFF_HEADS * 2 * DIFF_QK_DIM, DIFF_KV_HEADS * 2 * DIFF_QK_DIM, DIFF_KV_HEADS * DIFF_V_DIM,
                      MLA_Q_RANK, MLA_KV_RANK + MLA_ROPE_DIM, FOX_OUT, FOX_KV_HEADS * FOX_HEAD_DIM,
                      FOX_KV_HEADS * FOX_HEAD_DIM, FOX_HEADS])
    w = w_in[l]
    seg = lambda i: w[:, int(offs[i]):int(offs[i + 1])]
    wdq, wdk, wdv, wmq, wmkv, wfq, wfk, wfv, wfg = [seg(i) for i in range(9)]
    wckv, wkr = wmkv[:, :MLA_KV_RANK], wmkv[:, MLA_KV_RANK:]
    w_aug = jnp.concatenate([
        wdq, _swap_halves_cols(wdq), wdk, _swap_halves_cols(wdk), wdv, wmq, wckv,
        _pad_cols(wkr, LANES), _pad_cols(_swap_halves_cols(wkr), LANES), _pad_cols(wfg, LANES),
        wfq, wfk, wfv], axis=1).astype(BF16)
    assert w_aug.shape[1] == W_AUG_COLS
    wuq3 = w_uq[l].reshape(MLA_Q_RANK, MLA_HEADS, MLA_NOPE_DIM + MLA_ROPE_DIM)
    wuq_nope = wuq3[:, :, :MLA_NOPE_DIM].reshape(MLA_Q_RANK, MLA_HEADS * MLA_NOPE_DIM)
    wuq_rope = wuq3[:, :, MLA_NOPE_DIM:].reshape(MLA_Q_RANK, MLA_HEADS * MLA_ROPE_DIM)
    wuq_aug = jnp.concatenate([wuq_nope, wuq_rope, _swap_halves_cols(wuq_rope)], axis=1).astype(BF16)
    wuk3 = w_uk[l].reshape(MLA_KV_RANK, MLA_HEADS, MLA_NOPE_DIM)
    wukbd = _block_diag(jnp.transpose(wuk3, (1, 2, 0))).astype(BF16)
    wuv3 = w_uv[l].reshape(MLA_KV_RANK, MLA_HEADS, MLA_V_DIM)
    wuvbd = _block_diag(jnp.transpose(wuv3, (1, 0, 2))).astype(BF16)
    bf = _pad_cols(b_forget[l][None, :], LANES)
    mix_in_w = (w_aug, mla_q_norm_g[l][None, :], wuq_aug, wukbd, mla_kv_norm_g[l][None, :], bf)

    head_valid = jnp.concatenate([jnp.ones((DIFF_V_DIM,), F32), jnp.zeros((DIFF_V_DIM,), F32)])
    gd = jnp.tile(jnp.concatenate([diff_subln_g[l] * (1.0 - lam_init), jnp.zeros((DIFF_V_DIM,), F32)]),
                  DIFF_HEADS)[None, :]
    bmd = jnp.kron(jnp.eye(DIFF_HEADS, dtype=F32), jnp.outer(head_valid, head_valid) / DIFF_V_DIM).astype(BF16)
    wo = w_o[l]
    wod = wo[:DIFF_OUT].reshape(DIFF_HEADS, DIFF_V_DIM, D_MODEL)
    wod = jnp.concatenate([wod, jnp.zeros_like(wod)], axis=1).reshape(2 * DIFF_OUT, D_MODEL).astype(BF16)
    wom = wo[DIFF_OUT:DIFF_OUT + MLA_OUT].astype(BF16)
    wof = wo[DIFF_OUT + MLA_OUT:].astype(BF16)
    mix_out_w = (diff_lambda[l], gd, bmd, wuvbd, mla_out_g[l][None, :], fox_out_g[l][None, :], wod, wom, wof)
    return mix_in_w, mix_out_w


def _rope_tables(pos):
    half = DIFF_QK_DIM // 2
    inv_freq = ROPE_THETA ** (-jnp.arange(half, dtype=F32) / half)
    ang = pos.astype(F32)[:, None] * inv_freq[None, :]
    cos, sin = jnp.cos(ang), jnp.sin(ang)
    return (jnp.tile(jnp.concatenate([cos, cos], axis=1), (1, 8)),
            jnp.tile(jnp.concatenate([-sin, sin], axis=1), (1, 8)))


def kernel(x_prompt, x_sample, cache_diff_k, cache_diff_v, cache_mla_ckv, cache_mla_krope, cache_fox_k, cache_fox_v, cache_fox_logf, page_table, ln_g, ln_b, ffn_w1, ffn_w3, ffn_w2, w_in, b_forget, diff_lambda, diff_subln_g, mla_q_norm_g, w_uq, mla_kv_norm_g, w_uk, w_uv, mla_out_g, fox_out_g, w_o):
    depth = ffn_w1.shape[0]
    alpha = (2 * depth) ** 0.25
    bp, seq, _ = x_prompt.shape
    bs, n_dec, _ = x_sample.shape
    n_pages = page_table.shape[1]
    past_len = n_pages * PAGE_SIZE

    w1b, w3b, w2b = ffn_w1.astype(BF16), ffn_w3.astype(BF16), ffn_w2.astype(BF16)
    cos_p, sin_p = _rope_tables(jnp.arange(seq, dtype=jnp.int32))
    cos_s, sin_s = _rope_tables(past_len + jnp.arange(n_dec, dtype=jnp.int32))
    tm_s = min(MIX_TM, bs * n_dec)
    cos_s, sin_s = jnp.tile(cos_s, (tm_s // n_dec, 1)), jnp.tile(sin_s, (tm_s // n_dec, 1))

    caches = (jnp.transpose(cache_diff_k, (0, 1, 3, 4, 2)), jnp.transpose(cache_diff_v, (0, 1, 3, 4, 2)),
              cache_mla_ckv, jnp.transpose(cache_mla_krope, (0, 1, 3, 2)),
              jnp.transpose(cache_fox_k, (0, 1, 3, 4, 2)), jnp.transpose(cache_fox_v, (0, 1, 3, 4, 2)),
              jnp.transpose(cache_fox_logf, (0, 1, 3, 2)))

    xp = x_prompt.reshape(bp * seq, D_MODEL)
    xs = x_sample.reshape(bs * n_dec, D_MODEL)
    p_rows, s_rows = [], []
    for l in range(depth):
        lam_init = 0.8 - 0.6 * math.exp(-0.3 * l)
        mix_in_w, mix_out_w = _prep_layer_weights(
            l, w_in, b_forget, mla_q_norm_g, w_uq, mla_kv_norm_g, w_uk, w_uv, diff_lambda, diff_subln_g,
            mla_out_g, fox_out_g, w_o, lam_init)
        ffn = lambda x, j: _ffn_ln(x, w1b[l, j], w3b[l, j], w2b[l, j], ln_g[l, 2 * j][None, :],
                                   ln_b[l, 2 * j][None, :], alpha)

        xp = ffn(xp, 0)
        (qd, qm, fq, dkT, dvT, ckv, krT, fkT, fvT, lfT, kdT, vdT, kmT, kfT, vfT, cT) = _mix_in(
            xp, mix_in_w, cos_p, sin_p, transposed=True, batch=bp, seq_len=seq)
        od = _flash(qd, kdT, vdT, None, n_sub=2, split_keys=True, k_rows=2 * DIFF_QK_DIM,
                    k_idx=lambda h: h // (DIFF_HEADS // DIFF_KV_HEADS), dv=DIFF_V_DIM,
                    v_idx=lambda h: h // (DIFF_HEADS // DIFF_KV_HEADS), name="flash_diff")
        om = _flash(qm, kmT, kmT, None, n_sub=1, split_keys=False, k_rows=MLA_QK_PAD, k_idx=lambda h: 0,
                    dv=MLA_KV_RANK, v_idx=lambda h: 0, name="flash_mla")
        of = _flash(fq, kfT, vfT, cT, n_sub=FOX_HEADS // FOX_KV_HEADS, split_keys=False, k_rows=FOX_HEAD_DIM,
                    k_idx=lambda h: h, dv=FOX_HEAD_DIM, v_idx=lambda h: h, name="flash_fox")
        xp = _mix_out(xp, od, om, of,
                      mix_out_w + (ln_g[l, 1][None, :], ln_b[l, 1][None, :]), alpha, lam_init)
        xp = ffn(xp, 1)
        featT = lambda a, kvh: jnp.transpose(a.reshape(bp, kvh, a.shape[1] // kvh, seq), (0, 3, 1, 2))
        p_rows.append((featT(dkT, DIFF_KV_HEADS), featT(dvT, DIFF_KV_HEADS), ckv.reshape(bp, seq, MLA_KV_RANK),
                       jnp.transpose(krT, (0, 2, 1)), featT(fkT, FOX_KV_HEADS), featT(fvT, FOX_KV_HEADS),
                       jnp.transpose(lfT, (0, 2, 1))))

        xs = ffn(xs, 0)
        (qd, qlat, qrope, fq, dk, dv, ckv, kr, fk, fv, lf, c) = _mix_in(
            xs, mix_in_w, cos_s, sin_s, transposed=False, batch=bs, seq_len=n_dec)
        g_d = DIFF_HEADS // DIFF_KV_HEADS
        q6 = jnp.transpose(qd.reshape(bs, n_dec, DIFF_KV_HEADS, g_d, 2, DIFF_QK_DIM), (0, 2, 3, 4, 1, 5))
        qd_s = (q6[..., None, :] * jnp.eye(2, dtype=BF16)[None, None, None, :, None, :, None]).reshape(
            bs, DIFF_KV_HEADS, g_d * 2 * n_dec, 2 * DIFF_QK_DIM)
        qm_s = jnp.concatenate([qlat.reshape(bs, n_dec, MLA_HEADS, MLA_KV_RANK),
                                qrope.reshape(bs, n_dec, MLA_HEADS, MLA_ROPE_DIM)], axis=-1)
        qm_s = jnp.transpose(qm_s, (0, 2, 1, 3)).reshape(bs, MLA_HEADS * n_dec, MLA_QK)
        g_f = FOX_HEADS // FOX_KV_HEADS
        qf_s = jnp.transpose(fq.reshape(bs, n_dec, FOX_KV_HEADS, g_f, FOX_HEAD_DIM), (0, 2, 3, 1, 4)).reshape(
            bs, FOX_KV_HEADS, g_f * n_dec, FOX_HEAD_DIM)
        c4 = c[:, :FOX_HEADS].reshape(bs, n_dec, FOX_KV_HEADS, g_f)
        cqf = jnp.transpose(c4, (0, 2, 3, 1)).reshape(bs, FOX_KV_HEADS, g_f * n_dec, 1)
        ctn = jnp.transpose(c[:, :FOX_HEADS].reshape(bs, n_dec, FOX_HEADS), (0, 2, 1))
        per_seq = lambda a: a.reshape(bs, n_dec, a.shape[1])
        od, om, of = _sample_attn(l, page_table, qd_s, qm_s, qf_s, cqf,
                                  (per_seq(dk), per_seq(dv), per_seq(ckv), per_seq(kr), per_seq(fk), per_seq(fv), ctn),
                                  caches)
        od = jnp.transpose(od.reshape(bs, DIFF_KV_HEADS, g_d, 2, n_dec, DIFF_V_DIM), (0, 4, 1, 2, 3, 5)).reshape(
            bs * n_dec, 2 * DIFF_OUT)
        om = jnp.transpose(om.reshape(bs, MLA_HEADS, n_dec, MLA_KV_RANK), (0, 2, 1, 3)).reshape(
            bs * n_dec, MLA_HEADS * MLA_KV_RANK)
        of = jnp.transpose(of.reshape(bs, FOX_KV_HEADS, g_f, n_dec, FOX_HEAD_DIM), (0, 3, 1, 2, 4)).reshape(
            bs * n_dec, FOX_OUT)
        xs = _mix_out(xs, od, om, of, mix_out_w + (ln_g[l, 1][None, :], ln_b[l, 1][None, :]), alpha, lam_init)
        xs = ffn(xs, 1)
        s_rows.append((dk.reshape(bs, n_dec, DIFF_KV_HEADS, 2 * DIFF_QK_DIM),
                       dv.reshape(bs, n_dec, DIFF_KV_HEADS, DIFF_V_DIM),
                       ckv.reshape(bs, n_dec, MLA_KV_RANK), kr[:, :MLA_ROPE_DIM].reshape(bs, n_dec, MLA_ROPE_DIM),
                       fk.reshape(bs, n_dec, FOX_KV_HEADS, FOX_HEAD_DIM),
                       fv.reshape(bs, n_dec, FOX_KV_HEADS, FOX_HEAD_DIM),
                       lf[:, :FOX_HEADS].reshape(bs, n_dec, FOX_HEADS)))

    stack = lambda rows: tuple(jnp.stack([r[i] for r in rows], axis=0) for i in range(7))
    return (xp.reshape(bp, seq, D_MODEL), xs.reshape(bs, n_dec, D_MODEL)) + stack(p_rows) + stack(s_rows)
```

```python
import functools
import math

import numpy as np
import jax
import jax.numpy as jnp
from jax import lax
from jax.experimental import pallas as pl
from jax.experimental.pallas import tpu as pltpu

F32 = jnp.float32
BF16 = jnp.bfloat16

D_MODEL = 1024
DIFF_HEADS, DIFF_KV_HEADS, DIFF_QK_DIM, DIFF_V_DIM = 4, 2, 32, 64
MLA_HEADS, MLA_Q_RANK, MLA_KV_RANK, MLA_NOPE_DIM, MLA_ROPE_DIM, MLA_V_DIM = 8, 256, 128, 64, 32, 64
FOX_HEADS, FOX_KV_HEADS, FOX_HEAD_DIM = 4, 2, 64
D_FF = 2816
PAGE_SIZE = 128
ROPE_THETA = 10000.0
LN_EPS = 1e-5
RMS_EPS = 1e-6
MLA_SCALE = (MLA_NOPE_DIM + MLA_ROPE_DIM) ** -0.5
DIFF_OUT = DIFF_HEADS * DIFF_V_DIM
MLA_OUT = MLA_HEADS * MLA_V_DIM
FOX_OUT = FOX_HEADS * FOX_HEAD_DIM
MLA_QK = MLA_KV_RANK + MLA_ROPE_DIM
LOG2E = math.log2(math.e)
ONES_ROWS = 16
MLA_QK_PAD = MLA_KV_RANK + ONES_ROWS + MLA_ROPE_DIM

LANES = 128
VMEM_LIMIT_BYTES = 56 * 1024 * 1024

FFN_TM = 512
FFN_CHUNK = 256
MIX_TM = 512
ATT_T = 1024
PAGES_PER_CHUNK = 64

SEG_DQ, SEG_DQ_SW, SEG_DK, SEG_DK_SW, SEG_DV = 0, 256, 512, 640, 768
SEG_MQ, SEG_CKV, SEG_KR, SEG_KR_SW, SEG_FG = 896, 1152, 1280, 1408, 1536
SEG_FQ, SEG_FK, SEG_FV, W_AUG_COLS = 1664, 1920, 2048, 2176


def _ln(y, g, b):
    mu = jnp.mean(y, axis=-1, keepdims=True)
    yc = y - mu
    var = jnp.mean(yc * yc, axis=-1, keepdims=True)
    return yc * lax.rsqrt(var + LN_EPS) * g + b


def _rms(x, g):
    return x * lax.rsqrt(jnp.mean(x * x, axis=-1, keepdims=True) + RMS_EPS) * g


def _cparams(sem):
    return pltpu.CompilerParams(dimension_semantics=sem, vmem_limit_bytes=VMEM_LIMIT_BYTES)


def _const_spec(shape):
    nd = len(shape)
    return pl.BlockSpec(shape, lambda *_: (0,) * nd, pipeline_mode=pl.Buffered(1))


def _ffn_ln_kernel(x_ref, w1_ref, w3_ref, w2_ref, g_ref, b_ref, o_ref, *, alpha):
    x = x_ref[...]
    xb = x.astype(BF16)
    acc = jnp.zeros(x.shape, F32)
    for c in range(D_FF // FFN_CHUNK):
        sl = slice(c * FFN_CHUNK, (c + 1) * FFN_CHUNK)
        h1 = jnp.dot(xb, w1_ref[:, sl], preferred_element_type=F32)
        h3 = jnp.dot(xb, w3_ref[:, sl], preferred_element_type=F32)
        h = (h1 * jax.nn.sigmoid(h1) * h3).astype(BF16)
        acc = acc + jnp.dot(h, w2_ref[sl, :], preferred_element_type=F32)
    o_ref[...] = _ln(alpha * x + 0.5 * acc, g_ref[...], b_ref[...])


def _ffn_ln(x, w1, w3, w2, g, b, alpha):
    n = x.shape[0]
    tm = min(FFN_TM, n)
    return pl.pallas_call(
        functools.partial(_ffn_ln_kernel, alpha=alpha),
        grid=(n // tm,),
        in_specs=[
            pl.BlockSpec((tm, D_MODEL), lambda i: (i, 0)),
            _const_spec((D_MODEL, D_FF)), _const_spec((D_MODEL, D_FF)), _const_spec((D_FF, D_MODEL)),
            _const_spec((1, D_MODEL)), _const_spec((1, D_MODEL)),
        ],
        out_specs=pl.BlockSpec((tm, D_MODEL), lambda i: (i, 0)),
        out_shape=jax.ShapeDtypeStruct((n, D_MODEL), F32),
        compiler_params=_cparams(("parallel",)),
        name="ffn_ln",
    )(x, w1, w3, w2, g, b)


def _log_sigmoid(z):
    return jnp.minimum(z, 0.0) - jnp.log1p(jnp.exp(-jnp.abs(z)))


def _mix_in_kernel(x_ref, w_ref, cos_ref, sin_ref, gq_ref, wuq_ref, wukbd_ref, gkv_ref, bf_ref, *refs,
                   transposed, seq_len, tiles_per_seq):
    if transposed:
        (qd_ref, qm_ref, fq_ref, dkT_ref, dvT_ref, ckv_ref, krT_ref, fkT_ref, fvT_ref,
         lfT_ref, kdT_ref, vdT_ref, kmT_ref, kfT_ref, vfT_ref, cT_ref, carry_ref) = refs
    else:
        (qd_ref, qlat_ref, qrope_ref, fq_ref, dk_ref, dv_ref, ckv_ref, kr_ref, fk_ref, fv_ref,
         lf_ref, c_ref) = refs
    tm = x_ref.shape[0]
    xb = x_ref[...].astype(BF16)
    proj = jnp.dot(xb, w_ref[...], preferred_element_type=F32)
    cos = cos_ref[...]
    sin = sin_ref[...]
    cos1, sin1 = cos[:, :LANES], sin[:, :LANES]

    dq = proj[:, SEG_DQ:SEG_DQ + 256] * cos + proj[:, SEG_DQ_SW:SEG_DQ_SW + 256] * sin
    dk = proj[:, SEG_DK:SEG_DK + 128] * cos1 + proj[:, SEG_DK_SW:SEG_DK_SW + 128] * sin1
    dv = proj[:, SEG_DV:SEG_DV + 128]
    kr = proj[:, SEG_KR:SEG_KR + 128] * cos1 + proj[:, SEG_KR_SW:SEG_KR_SW + 128] * sin1
    fk = proj[:, SEG_FK:SEG_FK + 128]
    fv = proj[:, SEG_FV:SEG_FV + 128]

    mqn = _rms(proj[:, SEG_MQ:SEG_MQ + MLA_Q_RANK], gq_ref[...]).astype(BF16)
    q2 = jnp.dot(mqn, wuq_ref[...], preferred_element_type=F32)
    qrope = q2[:, 512:768] * cos + q2[:, 768:1024] * sin
    qlat = jnp.dot(q2[:, :512].astype(BF16), wukbd_ref[...], preferred_element_type=F32)
    ckv = _rms(proj[:, SEG_CKV:SEG_CKV + MLA_KV_RANK], gkv_ref[...])

    lane = lax.broadcasted_iota(jnp.int32, (tm, LANES), 1)
    lf = jnp.where(lane < FOX_HEADS, _log_sigmoid(proj[:, SEG_FG:SEG_FG + 128] + bf_ref[...]), 0.0)

    qd = (dq * (DIFF_QK_DIM ** -0.5 * LOG2E)).astype(BF16)
    qlat = (qlat * (MLA_SCALE * LOG2E)).astype(BF16)
    qrope = (qrope * (MLA_SCALE * LOG2E)).astype(BF16)
    fq = (proj[:, SEG_FQ:SEG_FQ + 256] * (FOX_HEAD_DIM ** -0.5 * LOG2E)).astype(BF16)
    ckv_ref[...] = ckv

    if transposed:
        for h in range(2 * DIFF_HEADS):
            qd_ref[h] = qd[:, h * DIFF_QK_DIM:(h + 1) * DIFF_QK_DIM]
        for h in range(FOX_HEADS):
            fq_ref[h] = fq[:, h * FOX_HEAD_DIM:(h + 1) * FOX_HEAD_DIM]
        zpad = jnp.zeros((tm, ONES_ROWS), BF16)
        for h in range(MLA_HEADS):
            qm_ref[h] = jnp.concatenate(
                [qlat[:, h * MLA_KV_RANK:(h + 1) * MLA_KV_RANK], zpad,
                 qrope[:, h * MLA_ROPE_DIM:(h + 1) * MLA_ROPE_DIM]], axis=1)

        dkT, dvT, fkT, fvT = dk.T, dv.T, fk.T, fv.T
        krT = kr.T[:MLA_ROPE_DIM]
        lfT = lf.T[:8]
        dkT_ref[...] = dkT
        dvT_ref[...] = dvT
        fkT_ref[...] = fkT
        fvT_ref[...] = fvT
        krT_ref[...] = krT
        lfT_ref[...] = lfT[:FOX_HEADS]
        ones_rows = jnp.where(lax.broadcasted_iota(jnp.int32, (ONES_ROWS, tm), 0) == 0, 1.0, 0.0).astype(BF16)
        kdT_ref[...] = dkT.astype(BF16)
        kfT_ref[...] = fkT.astype(BF16)
        for src, dst, width in ((dvT, vdT_ref, DIFF_V_DIM), (fvT, vfT_ref, FOX_HEAD_DIM)):
            for h in range(src.shape[0] // width):
                base = h * (width + ONES_ROWS)
                dst[base:base + width, :] = src[h * width:(h + 1) * width].astype(BF16)
                dst[base + width:base + width + ONES_ROWS, :] = ones_rows
        kmT_ref[0:MLA_KV_RANK, :] = ckv.T.astype(BF16)
        kmT_ref[MLA_KV_RANK:MLA_KV_RANK + ONES_ROWS, :] = ones_rows
        kmT_ref[MLA_KV_RANK + ONES_ROWS:MLA_QK_PAD, :] = krT.astype(BF16)

        @pl.when(pl.program_id(0) % tiles_per_seq == 0)
        def _():
            carry_ref[...] = jnp.zeros_like(carry_ref)
        lane8 = lax.broadcasted_iota(jnp.int32, (8, LANES), 1)
        carry = carry_ref[:, 0:1]
        for p in range(tm // LANES):
            piece = lfT[:, p * LANES:(p + 1) * LANES]
            total = jnp.sum(piece, axis=1, keepdims=True)
            k = 1
            while k < LANES:
                piece = piece + jnp.where(lane8 >= k, pltpu.roll(piece, k, 1), 0.0)
                k *= 2
            piece = piece + carry
            for h in range(FOX_HEADS):
                cT_ref[h, :, p * LANES:(p + 1) * LANES] = piece[h:h + 1, :]
            carry = carry + total
        carry_ref[...] = jnp.broadcast_to(carry, carry_ref.shape)
    else:
        qd_ref[...] = qd
        qlat_ref[...] = qlat
        qrope_ref[...] = qrope
        fq_ref[...] = fq
        dk_ref[...] = dk
        dv_ref[...] = dv
        fk_ref[...] = fk
        fv_ref[...] = fv
        kr_ref[...] = kr
        lf_ref[...] = lf
        row = lax.broadcasted_iota(jnp.int32, (tm, LANES), 0) % seq_len
        c = lf
        k = 1
        while k < seq_len:
            c = c + jnp.where(row >= k, pltpu.roll(c, k, 0), 0.0)
            k *= 2
        c_ref[...] = c


def _mix_in(x, wts, cos_t, sin_t, *, transposed, batch, seq_len):
    n = x.shape[0]
    tm = min(MIX_TM, n)
    nt = n // tm
    tok = lambda w: pl.BlockSpec((tm, w), lambda i: (i, 0))
    tok_out = lambda w, dt: jax.ShapeDtypeStruct((n, w), dt)
    if transposed:
        tps = seq_len // tm
        tbl = pl.BlockSpec((tm, 256), lambda i: (i % tps, 0))
        featT = lambda r: pl.BlockSpec((None, r, tm), lambda i: (i // tps, 0, i % tps))
        featT_out = lambda r: jax.ShapeDtypeStruct((batch, r, seq_len), F32)
        kb = min(ATT_T, seq_len) // tm
        blkT = lambda r: pl.BlockSpec((None, None, r, tm), lambda i: (i // tps, (i % tps) // kb, 0, (i % tps) % kb))
        blkT_out = lambda r: jax.ShapeDtypeStruct((batch, tps // kb, r, kb * tm), BF16)
        headq = lambda h, d: pl.BlockSpec((None, h, tm, d), lambda i: (i // tps, 0, i % tps, 0))
        headq_out = lambda h, d: jax.ShapeDtypeStruct((batch, h, seq_len, d), BF16)
        v_rows = lambda kvh, d: kvh * (d + ONES_ROWS)
        out_specs = [headq(2 * DIFF_HEADS, DIFF_QK_DIM), headq(MLA_HEADS, MLA_QK_PAD), headq(FOX_HEADS, FOX_HEAD_DIM),
                     featT(128), featT(128), tok(128), featT(MLA_ROPE_DIM), featT(128), featT(128),
                     featT(FOX_HEADS),
                     blkT(128), blkT(v_rows(DIFF_KV_HEADS, DIFF_V_DIM)), blkT(MLA_QK_PAD), blkT(128),
                     blkT(v_rows(FOX_KV_HEADS, FOX_HEAD_DIM)),
                     pl.BlockSpec((None, FOX_HEADS, None, 1, tm),
                                  lambda i: (i // tps, 0, (i % tps) // kb, 0, (i % tps) % kb))]
        out_shape = [headq_out(2 * DIFF_HEADS, DIFF_QK_DIM), headq_out(MLA_HEADS, MLA_QK_PAD),
                     headq_out(FOX_HEADS, FOX_HEAD_DIM),
                     featT_out(128), featT_out(128), tok_out(128, F32), featT_out(MLA_ROPE_DIM),
                     featT_out(128), featT_out(128), featT_out(FOX_HEADS),
                     blkT_out(128), blkT_out(v_rows(DIFF_KV_HEADS, DIFF_V_DIM)), blkT_out(MLA_QK_PAD), blkT_out(128),
                     blkT_out(v_rows(FOX_KV_HEADS, FOX_HEAD_DIM)),
                     jax.ShapeDtypeStruct((batch, FOX_HEADS, tps // kb, 1, kb * tm), F32)]
        scratch = [pltpu.VMEM((8, LANES), F32)]
        sem = ("arbitrary",)
    else:
        tps = 1
        tbl = pl.BlockSpec((tm, 256), lambda i: (0, 0))
        out_specs = [tok(256), tok(1024), tok(256), tok(256)] + [tok(128)] * 8
        out_shape = [tok_out(256, BF16), tok_out(1024, BF16), tok_out(256, BF16), tok_out(256, BF16)] + \
                    [tok_out(128, F32)] * 8
        scratch = []
        sem = ("parallel",)
    w_aug, gq, wuq, wukbd, gkv, bf = wts
    return pl.pallas_call(
        functools.partial(_mix_in_kernel, transposed=transposed, seq_len=seq_len, tiles_per_seq=tps),
        grid=(nt,),
        in_specs=[tok(D_MODEL), _const_spec(w_aug.shape), tbl, tbl, _const_spec(gq.shape),
                  _const_spec(wuq.shape), _const_spec(wukbd.shape), _const_spec(gkv.shape),
                  _const_spec(bf.shape)],
        out_specs=out_specs,
        out_shape=out_shape,
        scratch_shapes=scratch,
        compiler_params=_cparams(sem),
        name="mix_in_prompt" if transposed else "mix_in_sample",
    )(x, w_aug, cos_t, sin_t, gq, wuq, wukbd, gkv, bf)


def _flash_kernel(q_ref, kT_ref, vT_ref, *refs, has_bias, split_keys, dv):
    if has_bias:
        cT_ref, o_ref = refs
    else:
        (o_ref,) = refs
    i = pl.program_id(2)
    n_sub, t, _ = q_ref.shape
    k_rows = kT_ref.shape[1] // n_sub if split_keys else kT_ref.shape[1]
    qs = [q_ref[g] for g in range(n_sub)]
    row = lax.broadcasted_iota(jnp.int32, (t, t), 0)
    col = lax.broadcasted_iota(jnp.int32, (t, t), 1)
    if has_bias:
        cq = [LOG2E * jnp.sum(jnp.where(row == col, cT_ref[g, i], 0.0), axis=1, keepdims=True)
              for g in range(n_sub)]

    def step(j, carry, masked):
        kT = kT_ref[j]
        vT = vT_ref[j]
        out = []
        for g in range(n_sub):
            m, acc = carry[g]
            kg = kT[g * k_rows:(g + 1) * k_rows] if split_keys else kT
            s = jnp.dot(qs[g], kg, preferred_element_type=F32)
            if has_bias:
                s = s + (cq[g] - LOG2E * cT_ref[g, j])
            if masked:
                s = jnp.where(col <= row, s, -jnp.inf)
            m_new = jnp.maximum(m, jnp.max(s, axis=1, keepdims=True))
            p = jnp.exp2(s - m_new).astype(BF16)
            pv = lax.dot_general(p, vT, (((1,), (1,)), ((), ())), preferred_element_type=F32)
            out.append((m_new, jnp.exp2(m - m_new) * acc + pv))
        return tuple(out)

    init = tuple((jnp.full((t, 1), -jnp.inf, F32), jnp.zeros((t, dv + ONES_ROWS), F32)) for _ in range(n_sub))
    carry = lax.fori_loop(0, i, functools.partial(step, masked=False), init)
    res = step(i, carry, True)
    outs = [acc[:, :dv] / acc[:, dv:dv + 1] for _, acc in res]
    o_ref[...] = outs[0] if n_sub == 1 else jnp.concatenate(outs, axis=1)


def _flash(q, kT, vT, cT, *, n_sub, split_keys, k_rows, k_idx, dv, v_idx, name):
    b, hn, seq, dq = q.shape
    nkb, t = kT.shape[1], kT.shape[3]
    hg_n = hn // n_sub
    nq = seq // t
    in_specs = [
        pl.BlockSpec((None, n_sub, t, dq), lambda bi, h, i: (bi, h, i, 0)),
        pl.BlockSpec((None, nkb, k_rows, t), lambda bi, h, i: (bi, 0, k_idx(h), 0)),
        pl.BlockSpec((None, nkb, dv + ONES_ROWS, t), lambda bi, h, i: (bi, 0, v_idx(h), 0)),
    ]
    args = [q, kT, vT]
    if cT is not None:
        in_specs.append(pl.BlockSpec((None, n_sub, nkb, 1, t), lambda bi, h, i: (bi, h, 0, 0, 0)))
        args.append(cT)
    return pl.pallas_call(
        functools.partial(_flash_kernel, has_bias=cT is not None, split_keys=split_keys, dv=dv),
        grid=(b, hg_n, nq),
        in_specs=in_specs,
        out_specs=pl.BlockSpec((t, n_sub * dv), lambda bi, h, i: (bi * nq + i, h)),
        out_shape=jax.ShapeDtypeStruct((b * seq, hn * dv), F32),
        compiler_params=_cparams(("parallel", "parallel", "parallel")),
        name=name,
    )(*args)


def _softmax_update(s, state, v_dot):
    m, l, acc = state
    m_new = jnp.maximum(m, jnp.max(s, axis=1, keepdims=True))
    a = jnp.exp2(m - m_new)
    p = jnp.exp2(s - m_new)
    return m_new, a * l + jnp.sum(p, axis=1, keepdims=True), a * acc + v_dot(p.astype(BF16))


def _nt(a, b):
    return lax.dot_general(a, b, (((1,), (1,)), ((), ())), preferred_element_type=F32)


def _sample_attn_kernel(pt_ref,
                        qd_ref, qm_ref, qf_ref, cqf_ref,
                        dkn_ref, dvn_ref, ckvn_ref, krn_ref, fkn_ref, fvn_ref, ctn_ref,
                        dkc, dvc, ckvc, krc, fkc, fvc, lfc,
                        od_ref, om_ref, of_ref,
                        *scratch, layer, n_chunks, ppc):
    n_arrays = 7
    caches = (dkc, dvc, ckvc, krc, fkc, fvc, lfc)
    bufs = (scratch[:n_arrays], scratch[n_arrays:2 * n_arrays])
    sems = scratch[2 * n_arrays]
    b = pl.program_id(0)
    nb = pl.num_programs(0)
    n_dec = ctn_ref.shape[1]
    n_pairs = n_chunks // 2

    def page_dst(a, buf, p):
        rows = pl.ds(p * PAGE_SIZE, PAGE_SIZE)
        if a == 2:
            return buf.at[rows, :]
        return buf.at[:, rows] if buf.ndim == 2 else buf.at[:, :, rows]

    def start_chunk(bi, ci, slot):
        for p in range(ppc):
            page = pt_ref[bi, ci * ppc + p]
            for a in range(n_arrays):
                pltpu.make_async_copy(caches[a].at[layer, page], page_dst(a, bufs[slot][a], p),
                                      sems.at[slot, a]).start()

    def wait_chunk(slot):
        for a in range(n_arrays):
            buf = bufs[slot][a]
            pltpu.make_async_copy(buf, buf, sems.at[slot, a]).wait()

    @pl.when(b == 0)
    def _():
        start_chunk(0, n_chunks - 1, 0)

    qd = [qd_ref[h] for h in range(DIFF_KV_HEADS)]
    qm = qm_ref[...]
    qf = [qf_ref[h] for h in range(FOX_KV_HEADS)]
    cqf = [cqf_ref[h] for h in range(FOX_KV_HEADS)]
    g_fox = FOX_HEADS // FOX_KV_HEADS
    lane4 = lax.broadcasted_iota(jnp.int32, (FOX_HEADS, LANES), 1)

    def init(rows, dv):
        return (jnp.full((rows, 1), -jnp.inf, F32), jnp.zeros((rows, 1), F32), jnp.zeros((rows, dv), F32))

    def fox_bias(h, d_rows):
        return jnp.concatenate(
            [cqf[h][g * n_dec:(g + 1) * n_dec] + d_rows[h * g_fox + g:h * g_fox + g + 1, :] for g in range(g_fox)],
            axis=0) * LOG2E

    def compute_chunk(slot, carry):
        dk_buf, dv_buf, ckv_buf, kr_buf, fk_buf, fv_buf, lf_buf = bufs[slot]
        sd, sm, sf, lf_after = carry
        new_sd = []
        for h in range(DIFF_KV_HEADS):
            s = jnp.dot(qd[h], dk_buf[h].astype(BF16), preferred_element_type=F32)
            v = dv_buf[h].astype(BF16)
            new_sd.append(_softmax_update(s, sd[h], lambda p, v=v: _nt(p, v)))
        ckv = ckv_buf[...].astype(BF16)
        s = _nt(qm[:, :MLA_KV_RANK], ckv) + jnp.dot(qm[:, MLA_KV_RANK:], kr_buf[...].astype(BF16),
                                                   preferred_element_type=F32)
        new_sm = _softmax_update(s, sm, lambda p: jnp.dot(p, ckv, preferred_element_type=F32))
        pieces = [None] * ppc
        for p in reversed(range(ppc)):
            x = lf_buf[:, p * PAGE_SIZE:(p + 1) * PAGE_SIZE]
            total = jnp.sum(x, axis=1, keepdims=True)
            sfx = x
            kk = 1
            while kk < PAGE_SIZE:
                sfx = sfx + jnp.where(lane4 + kk < PAGE_SIZE, pltpu.roll(sfx, PAGE_SIZE - kk, 1), 0.0)
                kk *= 2
            pieces[p] = sfx - x + lf_after
            lf_after = lf_after + total
        d_rows = jnp.concatenate(pieces, axis=1)
        new_sf = []
        for h in range(FOX_KV_HEADS):
            s = jnp.dot(qf[h], fk_buf[h].astype(BF16), preferred_element_type=F32) + fox_bias(h, d_rows)
            v = fv_buf[h].astype(BF16)
            new_sf.append(_softmax_update(s, sf[h], lambda p, v=v: _nt(p, v)))
        return tuple(new_sd), new_sm, tuple(new_sf), lf_after

    def start_chunk_first(bi, ci, slot):
        @pl.when(b >= 0)
        def _():
            start_chunk(bi, ci, slot)

    def pair_body(j, carry):
        c0 = n_chunks - 1 - 2 * j
        wait_chunk(0)
        start_chunk_first(b, c0 - 1, 1)
        carry = compute_chunk(0, carry)
        wait_chunk(1)
        last = j + 1 == n_pairs
        start_chunk_first(jnp.where(last, jnp.where(b + 1 == nb, 0, b + 1), b),
                          jnp.where(last, n_chunks - 1, c0 - 2), 0)
        return compute_chunk(1, carry)

    rows_d = qd_ref.shape[1]
    rows_f = qf_ref.shape[1]
    carry0 = (tuple(init(rows_d, DIFF_V_DIM) for _ in range(DIFF_KV_HEADS)),
              init(qm_ref.shape[0], MLA_KV_RANK),
              tuple(init(rows_f, FOX_HEAD_DIM) for _ in range(FOX_KV_HEADS)),
              jnp.zeros((FOX_HEADS, 1), F32))
    sd, sm, sf, _ = lax.fori_loop(0, n_pairs, pair_body, carry0)

    @pl.when(b == nb - 1)
    def _():
        wait_chunk(0)

    def causal(rows):
        tq = lax.broadcasted_iota(jnp.int32, (rows, n_dec), 0) % n_dec
        tk = lax.broadcasted_iota(jnp.int32, (rows, n_dec), 1)
        return tk <= tq

    for h in range(DIFF_KV_HEADS):
        kn = dkn_ref[:, h * 64:(h + 1) * 64].astype(BF16)
        vn = dvn_ref[:, h * 64:(h + 1) * 64].astype(BF16)
        s = jnp.where(causal(rows_d), _nt(qd[h], kn), -jnp.inf)
        _, l, acc = _softmax_update(s, sd[h], lambda p, vn=vn: jnp.dot(p, vn, preferred_element_type=F32))
        od_ref[h] = acc / l
    ckvn = ckvn_ref[...].astype(BF16)
    s = _nt(qm[:, :MLA_KV_RANK], ckvn) + _nt(qm[:, MLA_KV_RANK:], krn_ref[:, :MLA_ROPE_DIM].astype(BF16))
    s = jnp.where(causal(qm_ref.shape[0]), s, -jnp.inf)
    _, l, acc = _softmax_update(s, sm, lambda p: jnp.dot(p, ckvn, preferred_element_type=F32))
    om_ref[...] = acc / l
    ctn = ctn_ref[...]
    for h in range(FOX_KV_HEADS):
        kn = fkn_ref[:, h * 64:(h + 1) * 64].astype(BF16)
        vn = fvn_ref[:, h * 64:(h + 1) * 64].astype(BF16)
        s = _nt(qf[h], kn) + fox_bias(h, -ctn)
        s = jnp.where(causal(rows_f), s, -jnp.inf)
        _, l, acc = _softmax_update(s, sf[h], lambda p, vn=vn: jnp.dot(p, vn, preferred_element_type=F32))
        of_ref[h] = acc / l


def _sample_attn(layer, page_table, qd, qm, qf, cqf, new_rows, caches):
    nb, n_pages = page_table.shape
    ppc = min(PAGES_PER_CHUNK, n_pages)
    n_chunks = n_pages // ppc
    ck = ppc * PAGE_SIZE
    n_dec = new_rows[0].shape[1]
    per_b = lambda shp: pl.BlockSpec((None,) + shp, lambda b, pt: (b,) + (0,) * len(shp))
    any_spec = pl.BlockSpec(memory_space=pl.ANY)
    rows_d, rows_m, rows_f = qd.shape[2], qm.shape[1], qf.shape[2]
    in_specs = [per_b(qd.shape[1:]), per_b(qm.shape[1:]), per_b(qf.shape[1:]), per_b(cqf.shape[1:])]
    in_specs += [per_b(a.shape[1:]) for a in new_rows]
    in_specs += [any_spec] * 7
    out_shape = [jax.ShapeDtypeStruct((nb, DIFF_KV_HEADS, rows_d, DIFF_V_DIM), F32),
                 jax.ShapeDtypeStruct((nb, rows_m, MLA_KV_RANK), F32),
                 jax.ShapeDtypeStruct((nb, FOX_KV_HEADS, rows_f, FOX_HEAD_DIM), F32)]
    out_specs = [per_b(s.shape[1:]) for s in out_shape]
    assert n_chunks % 2 == 0, "the chunk loop handles two chunks (one per buffer slot) per trip"
    slot_bufs = [
        pltpu.VMEM((DIFF_KV_HEADS, 64, ck), F32), pltpu.VMEM((DIFF_KV_HEADS, 64, ck), F32),
        pltpu.VMEM((ck, MLA_KV_RANK), F32), pltpu.VMEM((MLA_ROPE_DIM, ck), F32),
        pltpu.VMEM((FOX_KV_HEADS, 64, ck), F32), pltpu.VMEM((FOX_KV_HEADS, 64, ck), F32),
        pltpu.VMEM((FOX_HEADS, ck), F32),
    ]
    scratch = slot_bufs + slot_bufs + [pltpu.SemaphoreType.DMA((2, len(slot_bufs)))]
    return pl.pallas_call(
        functools.partial(_sample_attn_kernel, layer=layer, n_chunks=n_chunks, ppc=ppc),
        grid_spec=pltpu.PrefetchScalarGridSpec(
            num_scalar_prefetch=1, grid=(nb,), in_specs=in_specs, out_specs=out_specs,
            scratch_shapes=scratch),
        out_shape=out_shape,
        compiler_params=_cparams(("arbitrary",)),
        name="sample_attn",
    )(page_table, qd, qm, qf, cqf, *new_rows, *caches)


def _mix_out_kernel(x_ref, od_ref, om_ref, of_ref, dl_ref, gd_ref, bmd_ref, wuv_ref, gm_ref, gf_ref,
                    wod_ref, wom_ref, wof_ref, g_ref, b_ref, o_ref, *, alpha, lam_init):
    dl = dl_ref[...]
    lam = (jnp.exp(jnp.sum(dl[0:1] * dl[1:2], axis=1, keepdims=True))
           - jnp.exp(jnp.sum(dl[2:3] * dl[3:4], axis=1, keepdims=True)) + lam_init)
    od = od_ref[...]
    d = od - lam * pltpu.roll(od, od.shape[1] - DIFF_V_DIM, 1)
    ms = jnp.dot((d * d).astype(BF16), bmd_ref[...], preferred_element_type=F32)
    yd = d * lax.rsqrt(ms + RMS_EPS) * gd_ref[...]
    ov = jnp.dot(om_ref[...].astype(BF16), wuv_ref[...], preferred_element_type=F32)
    ym = _rms(ov, gm_ref[...])
    yf = _rms(of_ref[...], gf_ref[...])
    y = (jnp.dot(yd.astype(BF16), wod_ref[...], preferred_element_type=F32)
         + jnp.dot(ym.astype(BF16), wom_ref[...], preferred_element_type=F32)
         + jnp.dot(yf.astype(BF16), wof_ref[...], preferred_element_type=F32))
    o_ref[...] = _ln(alpha * x_ref[...] + y, g_ref[...], b_ref[...])


def _mix_out(x, od, om, of, wts, alpha, lam_init):
    n = x.shape[0]
    tm = min(MIX_TM, n)
    tok = lambda w: pl.BlockSpec((tm, w), lambda i: (i, 0))
    return pl.pallas_call(
        functools.partial(_mix_out_kernel, alpha=alpha, lam_init=lam_init),
        grid=(n // tm,),
        in_specs=[tok(D_MODEL), tok(od.shape[1]), tok(om.shape[1]), tok(of.shape[1])]
                 + [_const_spec(w.shape) for w in wts],
        out_specs=tok(D_MODEL),
        out_shape=jax.ShapeDtypeStruct((n, D_MODEL), F32),
        compiler_params=_cparams(("parallel",)),
        name="mix_out",
    )(x, od, om, of, *wts)


def _swap_halves_cols(w, chunk=32):
    c = w.shape[-1]
    w4 = w.reshape(w.shape[:-1] + (c // chunk, 2, chunk // 2))
    return jnp.flip(w4, axis=-2).reshape(w.shape)


def _pad_cols(w, width):
    return jnp.pad(w, [(0, 0)] * (w.ndim - 1) + [(0, width - w.shape[-1])])


def _block_diag(blocks):
    h, r, c = blocks.shape
    eye = jnp.eye(h, dtype=blocks.dtype)
    return (blocks[:, :, None, :] * eye[:, None, :, None]).reshape(h * r, h * c)


def _prep_layer_weights(l, w_in, b_forget, mla_q_norm_g, w_uq, mla_kv_norm_g, w_uk, w_uv, diff_lambda,
                        diff_subln_g, mla_out_g, fox_out_g, w_o, lam_init):
    offs = np.cumsum([0, DIFF_HEADS * 2 * DIFF_QK_DIM, DIFF_KV_HEADS * 2 * DIFF_QK_DIM, DIFF_KV_HEADS * DIFF_V_DIM,
                      MLA_Q_RANK, MLA_KV_RANK + MLA_ROPE_DIM, FOX_OUT, FOX_KV_HEADS * FOX_HEAD_DIM,
                      FOX_KV_HEADS * FOX_HEAD_DIM, FOX_HEADS])
    w = w_in[l]
    seg = lambda i: w[:, int(offs[i]):int(offs[i + 1])]
    wdq, wdk, wdv, wmq, wmkv, wfq, wfk, wfv, wfg = [seg(i) for i in range(9)]
    wckv, wkr = wmkv[:, :MLA_KV_RANK], wmkv[:, MLA_KV_RANK:]
    w_aug = jnp.concatenate([
        wdq, _swap_halves_cols(wdq), wdk, _swap_halves_cols(wdk), wdv, wmq, wckv,
        _pad_cols(wkr, LANES), _pad_cols(_swap_halves_cols(wkr), LANES), _pad_cols(wfg, LANES),
        wfq, wfk, wfv], axis=1).astype(BF16)
    assert w_aug.shape[1] == W_AUG_COLS
    wuq3 = w_uq[l].reshape(MLA_Q_RANK, MLA_HEADS, MLA_NOPE_DIM + MLA_ROPE_DIM)
    wuq_nope = wuq3[:, :, :MLA_NOPE_DIM].reshape(MLA_Q_RANK, MLA_HEADS * MLA_NOPE_DIM)
    wuq_rope = wuq3[:, :, MLA_NOPE_DIM:].reshape(MLA_Q_RANK, MLA_HEADS * MLA_ROPE_DIM)
    wuq_aug = jnp.concatenate([wuq_nope, wuq_rope, _swap_halves_cols(wuq_rope)], axis=1).astype(BF16)
    wuk3 = w_uk[l].reshape(MLA_KV_RANK, MLA_HEADS, MLA_NOPE_DIM)
    wukbd = _block_diag(jnp.transpose(wuk3, (1, 2, 0))).astype(BF16)
    wuv3 = w_uv[l].reshape(MLA_KV_RANK, MLA_HEADS, MLA_V_DIM)
    wuvbd = _block_diag(jnp.transpose(wuv3, (1, 0, 2))).astype(BF16)
    bf = _pad_cols(b_forget[l][None, :], LANES)
    mix_in_w = (w_aug, mla_q_norm_g[l][None, :], wuq_aug, wukbd, mla_kv_norm_g[l][None, :], bf)

    head_valid = jnp.concatenate([jnp.ones((DIFF_V_DIM,), F32), jnp.zeros((DIFF_V_DIM,), F32)])
    gd = jnp.tile(jnp.concatenate([diff_subln_g[l] * (1.0 - lam_init), jnp.zeros((DIFF_V_DIM,), F32)]),
                  DIFF_HEADS)[None, :]
    bmd = jnp.kron(jnp.eye(DIFF_HEADS, dtype=F32), jnp.outer(head_valid, head_valid) / DIFF_V_DIM).astype(BF16)
    wo = w_o[l]
    wod = wo[:DIFF_OUT].reshape(DIFF_HEADS, DIFF_V_DIM, D_MODEL)
    wod = jnp.concatenate([wod, jnp.zeros_like(wod)], axis=1).reshape(2 * DIFF_OUT, D_MODEL).astype(BF16)
    wom = wo[DIFF_OUT:DIFF_OUT + MLA_OUT].astype(BF16)
    wof = wo[DIFF_OUT + MLA_OUT:].astype(BF16)
    mix_out_w = (diff_lambda[l], gd, bmd, wuvbd, mla_out_g[l][None, :], fox_out_g[l][None, :], wod, wom, wof)
    return mix_in_w, mix_out_w


def _rope_tables(pos):
    half = DIFF_QK_DIM // 2
    inv_freq = ROPE_THETA ** (-jnp.arange(half, dtype=F32) / half)
    ang = pos.astype(F32)[:, None] * inv_freq[None, :]
    cos, sin = jnp.cos(ang), jnp.sin(ang)
    return (jnp.tile(jnp.concatenate([cos, cos], axis=1), (1, 8)),
            jnp.tile(jnp.concatenate([-sin, sin], axis=1), (1, 8)))


def kernel(x_prompt, x_sample, cache_diff_k, cache_diff_v, cache_mla_ckv, cache_mla_krope, cache_fox_k, cache_fox_v, cache_fox_logf, page_table, ln_g, ln_b, ffn_w1, ffn_w3, ffn_w2, w_in, b_forget, diff_lambda, diff_subln_g, mla_q_norm_g, w_uq, mla_kv_norm_g, w_uk, w_uv, mla_out_g, fox_out_g, w_o):
    depth = ffn_w1.shape[0]
    alpha = (2 * depth) ** 0.25
    bp, seq, _ = x_prompt.shape
    bs, n_dec, _ = x_sample.shape
    n_pages = page_table.shape[1]
    past_len = n_pages * PAGE_SIZE

    w1b, w3b, w2b = ffn_w1.astype(BF16), ffn_w3.astype(BF16), ffn_w2.astype(BF16)
    cos_p, sin_p = _rope_tables(jnp.arange(seq, dtype=jnp.int32))
    cos_s, sin_s = _rope_tables(past_len + jnp.arange(n_dec, dtype=jnp.int32))
    tm_s = min(MIX_TM, bs * n_dec)
    cos_s, sin_s = jnp.tile(cos_s, (tm_s // n_dec, 1)), jnp.tile(sin_s, (tm_s // n_dec, 1))

    caches = (jnp.transpose(cache_diff_k, (0, 1, 3, 4, 2)), jnp.transpose(cache_diff_v, (0, 1, 3, 4, 2)),
              cache_mla_ckv, jnp.transpose(cache_mla_krope, (0, 1, 3, 2)),
              jnp.transpose(cache_fox_k, (0, 1, 3, 4, 2)), jnp.transpose(cache_fox_v, (0, 1, 3, 4, 2)),
              jnp.transpose(cache_fox_logf, (0, 1, 3, 2)))

    xp = x_prompt.reshape(bp * seq, D_MODEL)
    xs = x_sample.reshape(bs * n_dec, D_MODEL)
    p_rows, s_rows = [], []
    for l in range(depth):
        lam_init = 0.8 - 0.6 * math.exp(-0.3 * l)
        mix_in_w, mix_out_w = _prep_layer_weights(
            l, w_in, b_forget, mla_q_norm_g, w_uq, mla_kv_norm_g, w_uk, w_uv, diff_lambda, diff_subln_g,
            mla_out_g, fox_out_g, w_o, lam_init)
        ffn = lambda x, j: _ffn_ln(x, w1b[l, j], w3b[l, j], w2b[l, j], ln_g[l, 2 * j][None, :],
                                   ln_b[l, 2 * j][None, :], alpha)

        xp = ffn(xp, 0)
        (qd, qm, fq, dkT, dvT, ckv, krT, fkT, fvT, lfT, kdT, vdT, kmT, kfT, vfT, cT) = _mix_in(
            xp, mix_in_w, cos_p, sin_p, transposed=True, batch=bp, seq_len=seq)
        od = _flash(qd, kdT, vdT, None, n_sub=2, split_keys=True, k_rows=2 * DIFF_QK_DIM,
                    k_idx=lambda h: h // (DIFF_HEADS // DIFF_KV_HEADS), dv=DIFF_V_DIM,
                    v_idx=lambda h: h // (DIFF_HEADS // DIFF_KV_HEADS), name="flash_diff")
        om = _flash(qm, kmT, kmT, None, n_sub=4, split_keys=False, k_rows=MLA_QK_PAD, k_idx=lambda h: 0,
                    dv=MLA_KV_RANK, v_idx=lambda h: 0, name="flash_mla")
        of = _flash(fq, kfT, vfT, cT, n_sub=FOX_HEADS // FOX_KV_HEADS, split_keys=False, k_rows=FOX_HEAD_DIM,
                    k_idx=lambda h: h, dv=FOX_HEAD_DIM, v_idx=lambda h: h, name="flash_fox")
        xp = _mix_out(xp, od, om, of,
                      mix_out_w + (ln_g[l, 1][None, :], ln_b[l, 1][None, :]), alpha, lam_init)
        xp = ffn(xp, 1)
        featT = lambda a, kvh: jnp.transpose(a.reshape(bp, kvh, a.shape[1] // kvh, seq), (0, 3, 1, 2))
        p_rows.append((featT(dkT, DIFF_KV_HEADS), featT(dvT, DIFF_KV_HEADS), ckv.reshape(bp, seq, MLA_KV_RANK),
                       jnp.transpose(krT, (0, 2, 1)), featT(fkT, FOX_KV_HEADS), featT(fvT, FOX_KV_HEADS),
                       jnp.transpose(lfT, (0, 2, 1))))

        xs = ffn(xs, 0)
        (qd, qlat, qrope, fq, dk, dv, ckv, kr, fk, fv, lf, c) = _mix_in(
            xs, mix_in_w, cos_s, sin_s, transposed=False, batch=bs, seq_len=n_dec)
        g_d = DIFF_HEADS // DIFF_KV_HEADS
        q6 = jnp.transpose(qd.reshape(bs, n_dec, DIFF_KV_HEADS, g_d, 2, DIFF_QK_DIM), (0, 2, 3, 4, 1, 5))
        qd_s = (q6[..., None, :] * jnp.eye(2, dtype=BF16)[None, None, None, :, None, :, None]).reshape(
            bs, DIFF_KV_HEADS, g_d * 2 * n_dec, 2 * DIFF_QK_DIM)
        qm_s = jnp.concatenate([qlat.reshape(bs, n_dec, MLA_HEADS, MLA_KV_RANK),
                                qrope.reshape(bs, n_dec, MLA_HEADS, MLA_ROPE_DIM)], axis=-1)
        qm_s = jnp.transpose(qm_s, (0, 2, 1, 3)).reshape(bs, MLA_HEADS * n_dec, MLA_QK)
        g_f = FOX_HEADS // FOX_KV_HEADS
        qf_s = jnp.transpose(fq.reshape(bs, n_dec, FOX_KV_HEADS, g_f, FOX_HEAD_DIM), (0, 2, 3, 1, 4)).reshape(
            bs, FOX_KV_HEADS, g_f * n_dec, FOX_HEAD_DIM)
        c4 = c[:, :FOX_HEADS].reshape(bs, n_dec, FOX_KV_HEADS, g_f)
        cqf = jnp.transpose(c4, (0, 2, 3, 1)).reshape(bs, FOX_KV_HEADS, g_f * n_dec, 1)
        ctn = jnp.transpose(c[:, :FOX_HEADS].reshape(bs, n_dec, FOX_HEADS), (0, 2, 1))
        per_seq = lambda a: a.reshape(bs, n_dec, a.shape[1])
        od, om, of = _sample_attn(l, page_table, qd_s, qm_s, qf_s, cqf,
                                  (per_seq(dk), per_seq(dv), per_seq(ckv), per_seq(kr), per_seq(fk), per_seq(fv), ctn),
                                  caches)
        od = jnp.transpose(od.reshape(bs, DIFF_KV_HEADS, g_d, 2, n_dec, DIFF_V_DIM), (0, 4, 1, 2, 3, 5)).reshape(
            bs * n_dec, 2 * DIFF_OUT)
        om = jnp.transpose(om.reshape(bs, MLA_HEADS, n_dec, MLA_KV_RANK), (0, 2, 1, 3)).reshape(
            bs * n_dec, MLA_HEADS * MLA_KV_RANK)
        of = jnp.transpose(of.reshape(bs, FOX_KV_HEADS, g_f, n_dec, FOX_HEAD_DIM), (0, 3, 1, 2, 4)).reshape(
            bs * n_dec, FOX_OUT)
        xs = _mix_out(xs, od, om, of, mix_out_w + (ln_g[l, 1][None, :], ln_b[l, 1][None, :]), alpha, lam_init)
        xs = ffn(xs, 1)
        s_rows.append((dk.reshape(bs, n_dec, DIFF_KV_HEADS, 2 * DIFF_QK_DIM),
                       dv.reshape(bs, n_dec, DIFF_KV_HEADS, DIFF_V_DIM),
                       ckv.reshape(bs, n_dec, MLA_KV_RANK), kr[:, :MLA_ROPE_DIM].reshape(bs, n_dec, MLA_ROPE_DIM),
                       fk.reshape(bs, n_dec, FOX_KV_HEADS, FOX_HEAD_DIM),
                       fv.reshape(bs, n_dec, FOX_KV_HEADS, FOX_HEAD_DIM),
                       lf[:, :FOX_HEADS].reshape(bs, n_dec, FOX_HEADS)))

    stack = lambda rows: tuple(jnp.stack([r[i] for r in rows], axis=0) for i in range(7))
    return (xp.reshape(bp, seq, D_MODEL), xs.reshape(bs, n_dec, D_MODEL)) + stack(p_rows) + stack(s_rows)
```
